```python
import jax, jax.numpy as jnp
from jax import lax
import numpy as np

D_MODEL = 1024
BATCH = 32
SEQ = 256
DEPTH = 2
DEC_BATCH = 8
DEC_SEQ = 4096
PAST_LEN = 256

GRID_W = 64
CONV_DIM = 512
CONV_WIDTH = 31
POOL_DIM = 512
POOL_WINDOWS = (2, 4, 8, 16)
POOL_GROUPS = 4
POOL_GROUP_DIM = POOL_DIM // POOL_GROUPS
MLA_HEADS = 8
NOPE_DIM = 64
ROPE_DIM = 32
QK_DIM = NOPE_DIM + ROPE_DIM
V_DIM = 64
Q_RANK = 256
KV_RANK = 128
ROPE_BASE = 10000.0
Q_BLOCK = 128
N_BRANCH = 3
SPLIT_IDX = (CONV_DIM, 2 * CONV_DIM, 2 * CONV_DIM + POOL_DIM, 2 * CONV_DIM + POOL_DIM + Q_RANK,
             2 * CONV_DIM + POOL_DIM + Q_RANK + KV_RANK,
             2 * CONV_DIM + POOL_DIM + Q_RANK + KV_RANK + ROPE_DIM)
IN_COLS = 2 * CONV_DIM + POOL_DIM + Q_RANK + KV_RANK + ROPE_DIM + N_BRANCH * D_MODEL
N_GROUPS = 4
EXP_PER_GROUP = 8
N_EXPERTS = N_GROUPS * EXP_PER_GROUP
TOP_K_IN_GROUP = 2
D_EXPERT = 256
EPS = 1e-6

kernel_name = "hybrid_diffusion_prefix_step"


def rms_norm(x, g):
    xf = x.astype(jnp.float32)
    y = xf * lax.rsqrt(jnp.mean(xf * xf, axis=-1, keepdims=True) + EPS)
    return (y * g.astype(jnp.float32)).astype(x.dtype)


def layer_norm(x, g, b):
    xf = x.astype(jnp.float32)
    mu = jnp.mean(xf, axis=-1, keepdims=True)
    var = jnp.mean(jnp.square(xf - mu), axis=-1, keepdims=True)
    y = (xf - mu) * lax.rsqrt(var + EPS)
    return (y * g.astype(jnp.float32) + b.astype(jnp.float32)).astype(x.dtype)


def axial_rope_tables(seq_len):
    rows = seq_len // GRID_W
    row = jnp.repeat(jnp.arange(rows), GRID_W).astype(jnp.float32)
    col = jnp.tile(jnp.arange(GRID_W), rows).astype(jnp.float32)
    half = ROPE_DIM // 2
    inv = 1.0 / (ROPE_BASE ** (jnp.arange(0, half, 2, dtype=jnp.float32) / half))
    ang = jnp.concatenate([row[:, None] * inv, col[:, None] * inv], axis=-1)
    return jnp.cos(ang), jnp.sin(ang)


def apply_axial_rope(x, cos, sin):
    B, S, H, _ = x.shape
    q4 = ROPE_DIM // 4
    xr = x.astype(jnp.float32).reshape(B, S, H, 2, 2, q4)
    x1, x2 = xr[..., 0, :], xr[..., 1, :]
    cs = cos.reshape(S, 1, 2, q4)
    sn = sin.reshape(S, 1, 2, q4)
    out = jnp.stack([x1 * cs - x2 * sn, x2 * cs + x1 * sn], axis=-2)
    return out.reshape(B, S, H, ROPE_DIM).astype(x.dtype)


def attend(q, k, v):
    B, Sq, H, Dk = q.shape
    nb = Sq // Q_BLOCK
    kf = k.astype(jnp.float32)
    vf = v.astype(jnp.float32)
    qb = q.astype(jnp.float32).reshape(B, nb, Q_BLOCK, H, Dk).swapaxes(0, 1)
    scale = QK_DIM ** -0.5

    def block(qi):
        s = jnp.einsum('bqhd,bkhd->bhqk', qi, kf) * scale
        p = jax.nn.softmax(s, axis=-1)
        return jnp.einsum('bhqk,bkhd->bqhd', p, vf)

    o = lax.map(block, qb)
    return o.swapaxes(0, 1).reshape(B, Sq, H * V_DIM).astype(q.dtype)


def conv_module(a, b, dw, dw_b, ln_g, ln_b, w_out):
    u = a * jax.nn.sigmoid(b)
    y = lax.conv_general_dilated(u, dw[:, None, :].astype(u.dtype), window_strides=(1,),
                                 padding=[(CONV_WIDTH // 2, CONV_WIDTH // 2)],
                                 dimension_numbers=('NWC', 'WIO', 'NWC'),
                                 feature_group_count=CONV_DIM) + dw_b
    y = jax.nn.silu(layer_norm(y, ln_g, ln_b))
    return y @ w_out


def pool_module(u, w_pool, scale, w_out):
    B, S, _ = u.shape
    uf = u.astype(jnp.float32).reshape(B, S, POOL_GROUPS, POOL_GROUP_DIM)
    cs = jnp.concatenate([jnp.zeros_like(uf[:, :1]), jnp.cumsum(uf, axis=1)], axis=1)
    t = jnp.arange(S)
    outs = []
    for g, w in enumerate(POOL_WINDOWS):
        lo = jnp.clip(t - w // 2, 0, S)
        hi = jnp.clip(t + w // 2, 0, S)
        csg = cs[:, :, g]
        mean = (csg[:, hi] - csg[:, lo]) / (hi - lo).astype(jnp.float32)[None, :, None]
        outs.append(mean - uf[:, :, g])
    m = jnp.stack(outs, axis=2)
    y = jnp.einsum('bsgc,gcd->bsgd', m, w_pool.astype(jnp.float32)).reshape(B, S, POOL_DIM)
    y = y * scale.astype(jnp.float32)
    return y.astype(u.dtype) @ w_out


def mla_keys(ckv, kr, w_ukv, k_g):
    B, S, _ = ckv.shape
    kv = (ckv @ w_ukv).reshape(B, S, MLA_HEADS, NOPE_DIM + V_DIM)
    k_nope, v = kv[..., :NOPE_DIM], kv[..., NOPE_DIM:]
    k = jnp.concatenate([k_nope, jnp.broadcast_to(kr[:, :, None, :], (B, S, MLA_HEADS, ROPE_DIM))], axis=-1)
    return rms_norm(k, k_g), v


def hier_moe(h, w_rg, b_rg, w_re, b_re, w_gate, w_up, w_down):
    B, S, _ = h.shape
    hf = h.astype(jnp.float32)
    pg = jax.nn.softmax(hf @ w_rg.astype(jnp.float32) + b_rg.astype(jnp.float32), axis=-1)
    g_prob, g_idx = lax.top_k(pg, 1)
    g_prob, g_idx = g_prob[..., 0], g_idx[..., 0]
    le = (hf @ w_re.astype(jnp.float32) + b_re.astype(jnp.float32)).reshape(B, S, N_GROUPS, EXP_PER_GROUP)
    le_sel = jnp.sum(le * jax.nn.one_hot(g_idx, N_GROUPS, dtype=jnp.float32)[..., None], axis=2)
    pe = jax.nn.softmax(le_sel, axis=-1)
    top_p, top_i = lax.top_k(pe, TOP_K_IN_GROUP)
    top_p = top_p / jnp.sum(top_p, axis=-1, keepdims=True)
    weight = g_prob[..., None] * top_p
    eid = g_idx[..., None] * EXP_PER_GROUP + top_i
    combine = jnp.sum(jax.nn.one_hot(eid, N_EXPERTS, dtype=jnp.float32) * weight[..., None], axis=-2)
    a = jnp.einsum('bsd,edf->bsef', h, w_gate)
    u = jnp.einsum('bsd,edf->bsef', h, w_up)
    hid = jax.nn.silu(a) * u * combine[..., None].astype(h.dtype)
    return jnp.einsum('bsef,efd->bsd', hid, w_down)


def trunk_layer(x, mod, p, ctx_ckv, ctx_kr, rope):
    shift1, scale1, gate1, shift2, scale2, gate2 = jnp.split(mod, 6, axis=-1)
    B, S, _ = x.shape
    h = rms_norm(x, p['norm1']) * (1 + scale1) + shift1
    proj = h @ p['w_in']
    conv_a, conv_b, pool_u, q_d, kv_d, k_r, gates = jnp.split(proj, SPLIT_IDX, axis=-1)
    y_conv = conv_module(conv_a, conv_b, p['conv_dw'], p['conv_dw_b'], p['conv_ln_g'], p['conv_ln_b'],
                         p['w_conv_out'])
    y_pool = pool_module(pool_u, p['w_pool'], p['pool_scale'], p['w_pool_out'])
    q = (rms_norm(q_d, p['q_norm']) @ p['w_uq']).reshape(B, S, MLA_HEADS, QK_DIM)
    q = rms_norm(q, p['q_head'])
    ckv = rms_norm(kv_d, p['kv_norm'])
    k, v = mla_keys(ckv, k_r, p['w_ukv'], p['k_head'])
    if rope is not None:
        cos, sin = rope
        q = jnp.concatenate([q[..., :NOPE_DIM], apply_axial_rope(q[..., NOPE_DIM:], cos, sin)], axis=-1)
        k = jnp.concatenate([k[..., :NOPE_DIM], apply_axial_rope(k[..., NOPE_DIM:], cos, sin)], axis=-1)
    if ctx_ckv is not None:
        kc, vc = mla_keys(ctx_ckv, ctx_kr, p['w_ukv'], p['k_head'])
        k = jnp.concatenate([k, kc], axis=1)
        v = jnp.concatenate([v, vc], axis=1)
    y_mla = attend(q, k, v) @ p['w_mla_out']
    g_conv, g_pool, g_mla = jnp.split(jax.nn.sigmoid(gates), N_BRANCH, axis=-1)
    mixed = (g_conv * y_conv + g_pool * y_pool + g_mla * y_mla) @ p['w_out']
    x = x + gate1 * mixed
    h2 = rms_norm(x, p['norm2']) * (1 + scale2) + shift2
    x = x + gate2 * hier_moe(h2, p['w_rg'], p['b_rg'], p['w_re'], p['b_re'],
                             p['w_exp_gate'], p['w_exp_up'], p['w_exp_down'])
    return x, ckv, k_r


def setup_inputs(seed: int = 0) -> dict:
    key = jax.random.key(seed)
    ks = iter(jax.random.split(key, 48))

    def nrm(shape, scale=1.0):
        return jax.random.normal(next(ks), shape, jnp.float32) * scale

    def gain(shape):
        return 1.0 + 0.02 * nrm(shape)

    D = D_MODEL
    L = DEPTH
    return {
        'x_prompt': nrm((BATCH, SEQ, D)),
        'x_sample': nrm((DEC_BATCH, DEC_SEQ, D)),
        'cache_ckv': nrm((DEC_BATCH, L, PAST_LEN, KV_RANK)),
        'cache_kr': nrm((DEC_BATCH, L, PAST_LEN, ROPE_DIM)),
        'c': nrm((DEC_BATCH, D)),
        'c_ctx': nrm((D,)),
        'norm1_g': gain((L, D)),
        'norm2_g': gain((L, D)),
        'w_ada': nrm((L, D, 6 * D), 0.5 * D ** -0.5),
        'b_ada': nrm((L, 6 * D), 0.01),
        'w_in': nrm((L, D, IN_COLS), D ** -0.5),
        'conv_dw': nrm((L, CONV_WIDTH, CONV_DIM), CONV_WIDTH ** -0.5),
        'conv_dw_b': nrm((L, CONV_DIM), 0.01),
        'conv_ln_g': gain((L, CONV_DIM)),
        'conv_ln_b': nrm((L, CONV_DIM), 0.01),
        'w_conv_out': nrm((L, CONV_DIM, D), CONV_DIM ** -0.5),
        'w_pool': nrm((L, POOL_GROUPS, POOL_GROUP_DIM, POOL_GROUP_DIM), POOL_GROUP_DIM ** -0.5),
        'pool_scale': gain((L, POOL_DIM)),
        'w_pool_out': nrm((L, POOL_DIM, D), POOL_DIM ** -0.5),
        'q_norm_g': gain((L, Q_RANK)),
        'w_uq': nrm((L, Q_RANK, MLA_HEADS * QK_DIM), Q_RANK ** -0.5),
        'kv_norm_g': gain((L, KV_RANK)),
        'w_ukv': nrm((L, KV_RANK, MLA_HEADS * (NOPE_DIM + V_DIM)), KV_RANK ** -0.5),
        'q_head_g': gain((L, QK_DIM)),
        'k_head_g': gain((L, QK_DIM)),
        'w_mla_out': nrm((L, MLA_HEADS * V_DIM, D), (MLA_HEADS * V_DIM) ** -0.5),
        'w_out': nrm((L, D, D), D ** -0.5),
        'w_router_g': nrm((L, D, N_GROUPS), D ** -0.5),
        'b_router_g': nrm((L, N_GROUPS), 0.01),
        'w_router_e': nrm((L, D, N_EXPERTS), D ** -0.5),
        'b_router_e': nrm((L, N_EXPERTS), 0.01),
        'w_exp_gate': nrm((L, N_EXPERTS, D, D_EXPERT), D ** -0.5),
        'w_exp_up': nrm((L, N_EXPERTS, D, D_EXPERT), D ** -0.5),
        'w_exp_down': nrm((L, N_EXPERTS, D_EXPERT, D), D_EXPERT ** -0.5),
    }


def reference(x_prompt, x_sample, cache_ckv, cache_kr, c, c_ctx, norm1_g, norm2_g, w_ada, b_ada, w_in,
              conv_dw, conv_dw_b, conv_ln_g, conv_ln_b, w_conv_out, w_pool, pool_scale, w_pool_out,
              q_norm_g, w_uq, kv_norm_g, w_ukv, q_head_g, k_head_g, w_mla_out, w_out,
              w_router_g, b_router_g, w_router_e, b_router_e, w_exp_gate, w_exp_up, w_exp_down):
    rope = axial_rope_tables(x_sample.shape[1])
    y_prompt = x_prompt
    y_sample = x_sample
    new_ckv, new_kr = [], []
    for l in range(DEPTH):
        p = dict(norm1=norm1_g[l], norm2=norm2_g[l], w_in=w_in[l],
                 conv_dw=conv_dw[l], conv_dw_b=conv_dw_b[l], conv_ln_g=conv_ln_g[l], conv_ln_b=conv_ln_b[l],
                 w_conv_out=w_conv_out[l], w_pool=w_pool[l], pool_scale=pool_scale[l], w_pool_out=w_pool_out[l],
                 q_norm=q_norm_g[l], w_uq=w_uq[l], kv_norm=kv_norm_g[l], w_ukv=w_ukv[l],
                 q_head=q_head_g[l], k_head=k_head_g[l], w_mla_out=w_mla_out[l], w_out=w_out[l],
                 w_rg=w_router_g[l], b_rg=b_router_g[l], w_re=w_router_e[l], b_re=b_router_e[l],
                 w_exp_gate=w_exp_gate[l], w_exp_up=w_exp_up[l], w_exp_down=w_exp_down[l])
        mod_ctx = (jax.nn.silu(c_ctx) @ w_ada[l] + b_ada[l])[None, None, :]
        y_prompt, ckv_l, kr_l = trunk_layer(y_prompt, mod_ctx, p, None, None, None)
        new_ckv.append(ckv_l)
        new_kr.append(kr_l)
        mod_lat = (jax.nn.silu(c) @ w_ada[l] + b_ada[l])[:, None, :]
        y_sample, _, _ = trunk_layer(y_sample, mod_lat, p, cache_ckv[:, l], cache_kr[:, l], rope)
    new_cache_ckv = jnp.stack(new_ckv, axis=1)
    new_cache_kr = jnp.stack(new_kr, axis=1)
    return (y_prompt, y_sample, new_cache_ckv, new_cache_kr)
```

```python
import functools
import math

import jax
import jax.numpy as jnp
from jax import lax
from jax.experimental import pallas as pl
from jax.experimental.pallas import tpu as pltpu

F32 = jnp.float32
BF16 = jnp.bfloat16

LANES = 128
HALO = 16
EPS = 1e-6
GRID_W = 64
CONV_DIM = 512
CONV_WIDTH = 31
POOL_DIM = 512
POOL_WINDOWS = (2, 4, 8, 16)
POOL_GROUP_DIM = 128
HEADS = 8
NOPE = 64
ROPE = 32
QK_DIM = NOPE + ROPE
V_DIM = 64
Q_RANK = 256
KV_RANK = 128
ROPE_BASE = 10000.0
N_GROUPS = 4
EXP_PER_GROUP = 8
N_EXPERTS = N_GROUPS * EXP_PER_GROUP
PROJ_A = 2048
N_GATE = 3
VMEM_LIMIT = 56 * 1024 * 1024
NEG_BIG = -1e30
Q_SCALE = (QK_DIM ** -0.5) * math.log2(math.e)


def _const_spec(shape):
    nd = len(shape)
    return pl.BlockSpec(shape, lambda *_: (0,) * nd)


def _rms(x, n):
    return lax.rsqrt(jnp.sum(x * x, axis=-1, keepdims=True) * (1.0 / n) + EPS)


def _rope(x, c, s1, s2):
    return x * c + pltpu.roll(x, 8, 1) * s1 + pltpu.roll(x, LANES - 8, 1) * s2


def _ada_kernel(c_ref, w_ref, b_ref, o_ref):
    c = c_ref[...]
    s = c * jax.nn.sigmoid(c)
    o_ref[0] = jnp.dot(s, w_ref[0], preferred_element_type=F32,
                       precision=lax.Precision.HIGHEST) + b_ref[0]


def _ada(cvec, w_ada, b_ada):
    L, D, D6 = w_ada.shape
    rows = cvec.shape[0]
    tn = 1536
    return pl.pallas_call(
        _ada_kernel,
        grid=(L, D6 // tn),
        in_specs=[pl.BlockSpec((rows, D), lambda l, j: (0, 0)),
                  pl.BlockSpec((1, D, tn), lambda l, j: (l, 0, j)),
                  pl.BlockSpec((1, 1, tn), lambda l, j: (l, 0, j))],
        out_specs=pl.BlockSpec((1, rows, tn), lambda l, j: (l, 0, j)),
        out_shape=jax.ShapeDtypeStruct((L, rows, D6), F32),
        compiler_params=pltpu.CompilerParams(
            dimension_semantics=("parallel", "parallel"), vmem_limit_bytes=VMEM_LIMIT),
        name="ada",
    )(cvec, w_ada, b_ada.reshape(L, 1, D6))


def _pre_kernel(*refs, use_rope):
    if use_rope:
        (x_ref, mod_ref, g1_ref, wa_ref, qng_ref, kvg_ref, wuq_ref, qgain_ref,
         c_ref, s1_ref, s2_ref, u_ref, pool_ref, q_ref, ckv_ref, krp_ref) = refs
    else:
        (x_ref, mod_ref, g1_ref, wa_ref, qng_ref, kvg_ref, wuq_ref, qgain_ref,
         u_ref, pool_ref, q_ref, ckv_ref, krp_ref) = refs
    x = x_ref[0]
    d = x.shape[-1]
    shift1 = mod_ref[0, 0:1, :]
    scale1 = mod_ref[0, 1:2, :]
    h = x * _rms(x, d) * g1_ref[...] * (1.0 + scale1) + shift1
    proj = jnp.dot(h.astype(BF16), wa_ref[...], preferred_element_type=F32)
    a = proj[:, 0:CONV_DIM]
    b = proj[:, CONV_DIM:2 * CONV_DIM]
    u_ref[0] = a * jax.nn.sigmoid(b)
    o = 2 * CONV_DIM
    pool_ref[0] = proj[:, o:o + POOL_DIM]
    o += POOL_DIM
    qd = proj[:, o:o + Q_RANK]
    o += Q_RANK
    kvd = proj[:, o:o + KV_RANK]
    o += KV_RANK
    krp_ref[0] = proj[:, o:o + LANES]
    ckv_ref[0] = kvd * _rms(kvd, KV_RANK) * kvg_ref[...]
    qn = (qd * _rms(qd, Q_RANK) * qng_ref[...]).astype(BF16)
    q = jnp.dot(qn, wuq_ref[...], preferred_element_type=F32)
    qgain = qgain_ref[...]
    for hd in range(HEADS):
        qh = q[:, hd * LANES:(hd + 1) * LANES]
        r = _rms(qh, QK_DIM) * Q_SCALE
        qg = qh * qgain
        if use_rope:
            qg = _rope(qg, c_ref[...], s1_ref[...], s2_ref[...])
        q_ref[0, :, hd * LANES:(hd + 1) * LANES] = (qg * r).astype(BF16)


def _pre(x, mod6, g1, wa, qng, kvg, wuq, qgain, rope_tabs, tm):
    B, S, D = x.shape
    Bm = mod6.shape[0]
    use_rope = rope_tabs is not None
    mod_idx = (lambda b, i: (b, 0, 0)) if Bm > 1 else (lambda b, i: (0, 0, 0))
    in_specs = [pl.BlockSpec((1, tm, D), lambda b, i: (b, i, 0)),
                pl.BlockSpec((1, 6, D), mod_idx),
                _const_spec((1, D)),
                _const_spec(wa.shape),
                _const_spec((1, Q_RANK)),
                _const_spec((1, KV_RANK)),
                _const_spec(wuq.shape),
                _const_spec((1, LANES))]
    args = [x, mod6, g1, wa, qng, kvg, wuq, qgain]
    if use_rope:
        in_specs += [pl.BlockSpec((tm, LANES), lambda b, i: (i, 0))] * 3
        args += list(rope_tabs)

    def tok(width):
        return pl.BlockSpec((1, tm, width), lambda b, i: (b, i, 0))

    return pl.pallas_call(
        functools.partial(_pre_kernel, use_rope=use_rope),
        grid=(B, S // tm),
        in_specs=in_specs,
        out_specs=[tok(CONV_DIM), tok(POOL_DIM), tok(HEADS * LANES), tok(KV_RANK), tok(LANES)],
        out_shape=[jax.ShapeDtypeStruct((B, S, CONV_DIM), F32),
                   jax.ShapeDtypeStruct((B, S, POOL_DIM), F32),
                   jax.ShapeDtypeStruct((B, S, HEADS * LANES), BF16),
                   jax.ShapeDtypeStruct((B, S, KV_RANK), F32),
                   jax.ShapeDtypeStruct((B, S, LANES), F32)],
        compiler_params=pltpu.CompilerParams(
            dimension_semantics=("parallel", "parallel"), vmem_limit_bytes=VMEM_LIMIT),
        name="pre",
    )(*args)


def _kv_kernel(*refs, use_rope):
    if use_rope:
        (ckv_ref, krp_ref, wuk_ref, wuv_ref, kgain_ref, c_ref, s1_ref, s2_ref,
         k_ref, v_ref) = refs
    else:
        ckv_ref, krp_ref, wuk_ref, wuv_ref, kgain_ref, k_ref, v_ref = refs
    ckv = ckv_ref[0].astype(BF16)
    kn = jnp.dot(ckv, wuk_ref[...], preferred_element_type=F32)
    v_ref[0] = jnp.dot(ckv, wuv_ref[...], preferred_element_type=F32).astype(BF16)
    krp = krp_ref[0]
    kgain = kgain_ref[...]
    ssr = jnp.sum(krp * krp, axis=-1, keepdims=True)
    krg = krp * kgain
    if use_rope:
        krg = _rope(krg, c_ref[...], s1_ref[...], s2_ref[...])
    for hd in range(HEADS):
        knh = kn[:, hd * LANES:(hd + 1) * LANES]
        ssq = jnp.sum(knh * knh, axis=-1, keepdims=True) + ssr
        r = lax.rsqrt(ssq * (1.0 / QK_DIM) + EPS)
        k_ref[0, :, hd * LANES:(hd + 1) * LANES] = ((knh * kgain + krg) * r).astype(BF16)


def _kv(ckv, krp, wuk, wuv, kgain, rope_tabs, tk):
    B, Sk, _ = ckv.shape
    use_rope = rope_tabs is not None
    in_specs = [pl.BlockSpec((1, tk, KV_RANK), lambda b, i: (b, i, 0)),
                pl.BlockSpec((1, tk, LANES), lambda b, i: (b, i, 0)),
                _const_spec(wuk.shape), _const_spec(wuv.shape), _const_spec((1, LANES))]
    args = [ckv, krp, wuk, wuv, kgain]
    if use_rope:
        in_specs += [pl.BlockSpec((tk, LANES), lambda b, i: (i, 0))] * 3
        args += list(rope_tabs)
    return pl.pallas_call(
        functools.partial(_kv_kernel, use_rope=use_rope),
        grid=(B, Sk // tk),
        in_specs=in_specs,
        out_specs=[pl.BlockSpec((1, tk, HEADS * LANES), lambda b, i: (b, i, 0)),
                   pl.BlockSpec((1, tk, HEADS * V_DIM), lambda b, i: (b, i, 0))],
        out_shape=[jax.ShapeDtypeStruct((B, Sk, HEADS * LANES), BF16),
                   jax.ShapeDtypeStruct((B, Sk, HEADS * V_DIM), BF16)],
        compiler_params=pltpu.CompilerParams(
            dimension_semantics=("parallel", "parallel"), vmem_limit_bytes=VMEM_LIMIT),
        name="kv",
    )(*args)


def _attn_kernel(q_ref, k_ref, v_ref, o_ref, *, tk, nk):
    tq = q_ref.shape[1]
    lane = lax.broadcasted_iota(jnp.int32, (tq, LANES), 1)
    for pair in range(HEADS // 2):
        outs = []
        for sub in range(2):
            hd = 2 * pair + sub
            qh = q_ref[0, :, hd * LANES:(hd + 1) * LANES]

            def body(c, carry, qh=qh, hd=hd, pair=pair):
                m, l, acc = carry
                start = pl.multiple_of(c * tk, tk)
                ks = k_ref[0, pl.ds(start, tk), hd * LANES:(hd + 1) * LANES]
                vs = v_ref[0, pl.ds(start, tk), pair * LANES:(pair + 1) * LANES]
                s = lax.dot_general(qh, ks, (((1,), (1,)), ((), ())),
                                    preferred_element_type=F32)
                m_new = jnp.maximum(m, jnp.max(s, axis=-1, keepdims=True))
                alpha = jnp.exp2(m - m_new)
                p = jnp.exp2(s - m_new)
                l = alpha * l + jnp.sum(p, axis=-1, keepdims=True)
                acc = alpha * acc + jnp.dot(p.astype(BF16), vs, preferred_element_type=F32)
                return m_new, l, acc

            init = (jnp.full((tq, 1), NEG_BIG, F32), jnp.zeros((tq, 1), F32),
                    jnp.zeros((tq, LANES), F32))
            m, l, acc = lax.fori_loop(0, nk, body, init)
            outs.append(acc / l)
        o_ref[0, :, pair * LANES:(pair + 1) * LANES] = jnp.where(
            lane < V_DIM, outs[0], outs[1]).astype(BF16)


def _attn(q, k, v, tq, tk):
    B, S, _ = q.shape
    Sk = k.shape[1]
    return pl.pallas_call(
        functools.partial(_attn_kernel, tk=tk, nk=Sk // tk),
        grid=(B, S // tq),
        in_specs=[pl.BlockSpec((1, tq, HEADS * LANES), lambda b, i: (b, i, 0)),
                  pl.BlockSpec((1, Sk, HEADS * LANES), lambda b, i: (b, 0, 0)),
                  pl.BlockSpec((1, Sk, HEADS * V_DIM), lambda b, i: (b, 0, 0))],
        out_specs=pl.BlockSpec((1, tq, HEADS * V_DIM), lambda b, i: (b, i, 0)),
        out_shape=jax.ShapeDtypeStruct((B, S, HEADS * V_DIM), BF16),
        compiler_params=pltpu.CompilerParams(
            dimension_semantics=("parallel", "arbitrary"), vmem_limit_bytes=VMEM_LIMIT),
        name="attn",
    )(q, k, v)


def _merge_kernel(x_ref, mod_ref, g1_ref, g2_ref,
                  u_ref, up_ref, un_ref, p_ref, pp_ref, pn_ref, o_ref,
                  wg_ref, dw_ref, dwb_ref, lng_ref, lnb_ref, wco_ref,
                  wpool_ref, pscale_ref, wpo_ref, wmo_ref, wout_ref,
                  wr1_ref, wr2_ref, br_ref,
                  x1_ref, h2_ref, comb_ref,
                  ubuf, pbuf, ybuf, mbuf, *, seq_len):
    i = pl.program_id(1)
    n_i = pl.num_programs(1)
    tm = x_ref.shape[1]
    d = x_ref.shape[2]
    x = x_ref[0]
    shift1 = mod_ref[0, 0:1, :]
    scale1 = mod_ref[0, 1:2, :]
    gate1 = mod_ref[0, 2:3, :]
    shift2 = mod_ref[0, 3:4, :]
    scale2 = mod_ref[0, 4:5, :]
    h = (x * _rms(x, d) * g1_ref[...] * (1.0 + scale1) + shift1).astype(BF16)
    gates = jax.nn.sigmoid(jnp.dot(h, wg_ref[...], preferred_element_type=F32))

    has_prev = (i > 0).astype(F32)
    has_next = (i < n_i - 1).astype(F32)
    ubuf[0:HALO, :] = up_ref[0] * has_prev
    ubuf[HALO:HALO + tm, :] = u_ref[0]
    ubuf[HALO + tm:, :] = un_ref[0] * has_next
    pbuf[0:HALO, :] = pp_ref[0] * has_prev
    pbuf[HALO:HALO + tm, :] = p_ref[0]
    pbuf[HALO + tm:, :] = pn_ref[0] * has_next

    rc = 32
    half = CONV_WIDTH // 2
    for r0 in range(0, tm, rc):
        acc = jnp.zeros((rc, CONV_DIM), F32) + dwb_ref[...]
        for k in range(CONV_WIDTH):
            acc = acc + dw_ref[k:k + 1, :] * ubuf[r0 + HALO - half + k:r0 + HALO - half + k + rc, :]
        mu = jnp.mean(acc, axis=-1, keepdims=True)
        cen = acc - mu
        var = jnp.mean(cen * cen, axis=-1, keepdims=True)
        yn = cen * lax.rsqrt(var + EPS) * lng_ref[...] + lnb_ref[...]
        ybuf[r0:r0 + rc, :] = (yn * jax.nn.sigmoid(yn)).astype(BF16)
    y_conv = jnp.dot(ybuf[...], wco_ref[...], preferred_element_type=F32)

    t = i * tm + lax.broadcasted_iota(jnp.int32, (tm, 1), 0)
    for g, w in enumerate(POOL_WINDOWS):
        cols = slice(g * POOL_GROUP_DIM, (g + 1) * POOL_GROUP_DIM)
        s = pbuf[HALO - w // 2:HALO - w // 2 + tm, cols]
        for j in range(1, w):
            s = s + pbuf[HALO - w // 2 + j:HALO - w // 2 + j + tm, cols]
        cnt = (jnp.minimum(t + w // 2, seq_len) - jnp.maximum(t - w // 2, 0)).astype(F32)
        mbuf[:, cols] = (s / cnt - pbuf[HALO:HALO + tm, cols]).astype(BF16)
    y_pool = jnp.dot(mbuf[...], wpool_ref[...], preferred_element_type=F32) * pscale_ref[...]
    y_pool = jnp.dot(y_pool.astype(BF16), wpo_ref[...], preferred_element_type=F32)

    y_mla = jnp.dot(o_ref[0], wmo_ref[...], preferred_element_type=F32)

    mixed = (gates[:, 0:d] * y_conv + gates[:, d:2 * d] * y_pool + gates[:, 2 * d:3 * d] * y_mla)
    x1 = x + gate1 * jnp.dot(mixed.astype(BF16), wout_ref[...], preferred_element_type=F32)
    x1_ref[0] = x1

    h2 = x1 * _rms(x1, d) * g2_ref[...] * (1.0 + scale2) + shift2
    hi = h2.astype(BF16)
    h2_ref[0] = hi
    lo = (h2 - hi.astype(F32)).astype(BF16)
    r1 = jnp.dot(hi, wr1_ref[...], preferred_element_type=F32)
    r2 = jnp.dot(lo, wr2_ref[...], preferred_element_type=F32)
    logits = r1[:, 0:LANES] + r1[:, LANES:2 * LANES] + r2 + br_ref[...]

    lane = lax.broadcasted_iota(jnp.int32, (tm, LANES), 1)
    is_g = (lane >= N_EXPERTS) & (lane < N_EXPERTS + N_GROUPS)
    lg = jnp.where(is_g, logits, NEG_BIG)
    mg = jnp.max(lg, axis=-1, keepdims=True)
    eg = jnp.where(is_g, jnp.exp(lg - mg), 0.0)
    pg = eg / jnp.sum(eg, axis=-1, keepdims=True)
    pg_max = jnp.max(pg, axis=-1, keepdims=True)
    g_idx = jnp.min(jnp.where(is_g & (pg == pg_max), lane, 4 * LANES), axis=-1,
                    keepdims=True) - N_EXPERTS
    in_grp = (lane < N_EXPERTS) & ((lane >> 3) == g_idx)
    le = jnp.where(in_grp, logits, NEG_BIG)
    me = jnp.max(le, axis=-1, keepdims=True)
    ee = jnp.where(in_grp, jnp.exp(le - me), 0.0)
    pe = ee / jnp.sum(ee, axis=-1, keepdims=True)
    pe_m = jnp.where(in_grp, pe, -1.0)
    p1 = jnp.max(pe_m, axis=-1, keepdims=True)
    i1 = jnp.min(jnp.where(pe_m == p1, lane, 4 * LANES), axis=-1, keepdims=True)
    pe_m2 = jnp.where(lane == i1, -1.0, pe_m)
    p2 = jnp.max(pe_m2, axis=-1, keepdims=True)
    i2 = jnp.min(jnp.where(pe_m2 == p2, lane, 4 * LANES), axis=-1, keepdims=True)
    tot = p1 + p2
    comb_ref[0] = jnp.where(lane == i1, pg_max * (p1 / tot),
                            jnp.where(lane == i2, pg_max * (p2 / tot), 0.0))


def _merge(x, mod6, g1, g2, u, pool, o, w, tm):
    B, S, D = x.shape
    Bm = mod6.shape[0]
    nh = tm // HALO
    last_h = S // HALO - 1
    mod_idx = (lambda b, i: (b, 0, 0)) if Bm > 1 else (lambda b, i: (0, 0, 0))

    def tok(width):
        return pl.BlockSpec((1, tm, width), lambda b, i: (b, i, 0))

    def prev(width):
        return pl.BlockSpec((1, HALO, width), lambda b, i: (b, jnp.maximum(i * nh - 1, 0), 0))

    def nxt(width):
        return pl.BlockSpec((1, HALO, width), lambda b, i: (b, jnp.minimum((i + 1) * nh, last_h), 0))

    weights = [w["wg"], w["dw"], w["dwb"], w["lng"], w["lnb"], w["wco"], w["wpool"], w["pscale"],
               w["wpo"], w["wmo"], w["wout"], w["wr1"], w["wr2"], w["br"]]
    in_specs = ([tok(D), pl.BlockSpec((1, 6, D), mod_idx), _const_spec((1, D)), _const_spec((1, D)),
                 tok(CONV_DIM), prev(CONV_DIM), nxt(CONV_DIM),
                 tok(POOL_DIM), prev(POOL_DIM), nxt(POOL_DIM), tok(HEADS * V_DIM)]
                + [_const_spec(a.shape) for a in weights])
    return pl.pallas_call(
        functools.partial(_merge_kernel, seq_len=S),
        grid=(B, S // tm),
        in_specs=in_specs,
        out_specs=[tok(D), tok(D), tok(LANES)],
        out_shape=[jax.ShapeDtypeStruct((B, S, D), F32),
                   jax.ShapeDtypeStruct((B, S, D), BF16),
                   jax.ShapeDtypeStruct((B, S, LANES), F32)],
        scratch_shapes=[pltpu.VMEM((tm + 2 * HALO, CONV_DIM), F32),
                        pltpu.VMEM((tm + 2 * HALO, POOL_DIM), F32),
                        pltpu.VMEM((tm, CONV_DIM), BF16),
                        pltpu.VMEM((tm, POOL_DIM), BF16)],
        compiler_params=pltpu.CompilerParams(
            dimension_semantics=("parallel", "arbitrary"), vmem_limit_bytes=VMEM_LIMIT),
        name="merge",
    )(x, mod6, g1, g2, u, u, u, pool, pool, pool, o, *weights)


def _moe_kernel(h2_ref, comb_ref, x1_ref, mod_ref, wg_ref, wu_ref, wd_ref, o_ref, acc_ref):
    e = pl.program_id(1)

    @pl.when(e == 0)
    def _():
        acc_ref[...] = jnp.zeros_like(acc_ref)

    hb = h2_ref[...]
    a = jnp.dot(hb, wg_ref[0], preferred_element_type=F32)
    u = jnp.dot(hb, wu_ref[0], preferred_element_type=F32)
    lane = lax.broadcasted_iota(jnp.int32, comb_ref.shape, 1)
    c = jnp.sum(jnp.where(lane == e, comb_ref[...], 0.0), axis=-1, keepdims=True)
    hid = (a * jax.nn.sigmoid(a)) * u * c
    acc_ref[...] += jnp.dot(hid.astype(BF16), wd_ref[0], preferred_element_type=F32)

    @pl.when(e == pl.num_programs(1) - 1)
    def _():
        o_ref[...] = x1_ref[...] + mod_ref[0, 5:6, :] * acc_ref[...]


def _moe(h2, comb, x1, mod6, wg, wu, wd, seq_len, tm):
    T, D = x1.shape
    E, _, F = wg.shape
    Bm = mod6.shape[0]
    per_seq = seq_len // tm
    mod_idx = (lambda i, e: (i // per_seq, 0, 0)) if Bm > 1 else (lambda i, e: (0, 0, 0))
    return pl.pallas_call(
        _moe_kernel,
        grid=(T // tm, E),
        in_specs=[pl.BlockSpec((tm, D), lambda i, e: (i, 0)),
                  pl.BlockSpec((tm, LANES), lambda i, e: (i, 0)),
                  pl.BlockSpec((tm, D), lambda i, e: (i, 0)),
                  pl.BlockSpec((1, 6, D), mod_idx),
                  pl.BlockSpec((1, D, F), lambda i, e: (e, 0, 0)),
                  pl.BlockSpec((1, D, F), lambda i, e: (e, 0, 0)),
                  pl.BlockSpec((1, F, D), lambda i, e: (e, 0, 0))],
        out_specs=pl.BlockSpec((tm, D), lambda i, e: (i, 0)),
        out_shape=jax.ShapeDtypeStruct((T, D), F32),
        scratch_shapes=[pltpu.VMEM((tm, D), F32)],
        compiler_params=pltpu.CompilerParams(
            dimension_semantics=("parallel", "arbitrary"), vmem_limit_bytes=VMEM_LIMIT),
        name="moe",
    )(h2, comb, x1, mod6, wg, wu, wd)


def _rope_tables(seq_len, n_ctx):
    rows = seq_len // GRID_W
    row = jnp.repeat(jnp.arange(rows), GRID_W).astype(F32)
    col = jnp.tile(jnp.arange(GRID_W), rows).astype(F32)
    half = ROPE // 2
    inv = 1.0 / (ROPE_BASE ** (jnp.arange(0, half, 2, dtype=F32) / half))
    ang = jnp.concatenate([row[:, None] * inv, col[:, None] * inv], axis=-1)
    cos, sin = jnp.cos(ang), jnp.sin(ang)
    q4 = ROPE // 4
    zero = jnp.zeros((seq_len, q4), F32)
    c_parts, s1_parts, s2_parts = [jnp.ones((seq_len, NOPE), F32)], [jnp.zeros((seq_len, NOPE), F32)], \
        [jnp.zeros((seq_len, NOPE), F32)]
    for hf in range(2):
        cs, sn = cos[:, hf * q4:(hf + 1) * q4], sin[:, hf * q4:(hf + 1) * q4]
        c_parts += [cs, cs]
        s1_parts += [zero, sn]
        s2_parts += [-sn, zero]
    pad = LANES - NOPE - ROPE
    c_parts.append(jnp.ones((seq_len, pad), F32))
    s1_parts.append(jnp.zeros((seq_len, pad), F32))
    s2_parts.append(jnp.zeros((seq_len, pad), F32))
    c, s1, s2 = (jnp.concatenate(p, axis=-1) for p in (c_parts, s1_parts, s2_parts))
    if n_ctx:
        c = jnp.concatenate([c, jnp.ones((n_ctx, LANES), F32)], axis=0)
        s1 = jnp.concatenate([s1, jnp.zeros((n_ctx, LANES), F32)], axis=0)
        s2 = jnp.concatenate([s2, jnp.zeros((n_ctx, LANES), F32)], axis=0)
    return c, s1, s2


def _layer_weights(l, p):
    d = p["w_in"].shape[1]
    w_in = p["w_in"][l]
    n_a = 2 * CONV_DIM + POOL_DIM + Q_RANK + KV_RANK
    wa = jnp.concatenate([w_in[:, :n_a], jnp.zeros((d, NOPE), F32), w_in[:, n_a:n_a + ROPE],
                          jnp.zeros((d, LANES - NOPE - ROPE), F32)], axis=1).astype(BF16)
    wg = w_in[:, n_a + ROPE:].astype(BF16)
    wuq = jnp.pad(p["w_uq"][l].reshape(Q_RANK, HEADS, QK_DIM),
                  ((0, 0), (0, 0), (0, LANES - QK_DIM))).reshape(Q_RANK, HEADS * LANES).astype(BF16)
    wukv = p["w_ukv"][l].reshape(KV_RANK, HEADS, NOPE + V_DIM)
    wuk = jnp.pad(wukv[:, :, :NOPE], ((0, 0), (0, 0), (0, LANES - NOPE))).reshape(
        KV_RANK, HEADS * LANES).astype(BF16)
    wuv = wukv[:, :, NOPE:].reshape(KV_RANK, HEADS * V_DIM).astype(BF16)
    pad_gain = lambda g: jnp.pad(g, (0, LANES - QK_DIM)).reshape(1, LANES)
    wpool = jnp.zeros((POOL_DIM, POOL_DIM), F32)
    for g in range(len(POOL_WINDOWS)):
        sl = slice(g * POOL_GROUP_DIM, (g + 1) * POOL_GROUP_DIM)
        wpool = wpool.at[sl, sl].set(p["w_pool"][l, g])
    wr = jnp.concatenate([p["w_router_e"][l], p["w_router_g"][l]], axis=1)
    wr = jnp.pad(wr, ((0, 0), (0, LANES - wr.shape[1])))
    wr_hi = wr.astype(BF16)
    wr_lo = (wr - wr_hi.astype(F32)).astype(BF16)
    br = jnp.concatenate([p["b_router_e"][l], p["b_router_g"][l]])
    br = jnp.pad(br, (0, LANES - br.shape[0])).reshape(1, LANES)
    return dict(
        g1=p["norm1_g"][l].reshape(1, d), g2=p["norm2_g"][l].reshape(1, d),
        wa=wa, wg=wg, wuq=wuq, wuk=wuk, wuv=wuv,
        qng=p["q_norm_g"][l].reshape(1, Q_RANK), kvg=p["kv_norm_g"][l].reshape(1, KV_RANK),
        qgain=pad_gain(p["q_head_g"][l]), kgain=pad_gain(p["k_head_g"][l]),
        dw=p["conv_dw"][l], dwb=p["conv_dw_b"][l].reshape(1, CONV_DIM),
        lng=p["conv_ln_g"][l].reshape(1, CONV_DIM), lnb=p["conv_ln_b"][l].reshape(1, CONV_DIM),
        wco=p["w_conv_out"][l].astype(BF16), wpool=wpool.astype(BF16),
        pscale=p["pool_scale"][l].reshape(1, POOL_DIM), wpo=p["w_pool_out"][l].astype(BF16),
        wmo=p["w_mla_out"][l].astype(BF16), wout=p["w_out"][l].astype(BF16),
        wr1=jnp.concatenate([wr_hi, wr_lo], axis=1), wr2=wr_hi, br=br,
        weg=p["w_exp_gate"][l].astype(BF16), weu=p["w_exp_up"][l].astype(BF16),
        wed=p["w_exp_down"][l].astype(BF16))


def _pick(n, pref):
    t = min(n, pref)
    while n % t:
        t //= 2
    return t


def _trunk_layer(x, mod6, w, ctx_ckv, ctx_krp, rope_q, rope_k):
    B, S, D = x.shape
    tm = _pick(S, 256)
    u, pool, q, ckv, krp = _pre(x, mod6, w["g1"], w["wa"], w["qng"], w["kvg"], w["wuq"], w["qgain"],
                                rope_q, tm)
    if ctx_ckv is not None:
        ckv_all = jnp.concatenate([ckv, ctx_ckv], axis=1)
        krp_all = jnp.concatenate([krp, ctx_krp], axis=1)
    else:
        ckv_all, krp_all = ckv, krp
    Sk = ckv_all.shape[1]
    tk = _pick(Sk, 256)
    k, v = _kv(ckv_all, krp_all, w["wuk"], w["wuv"], w["kgain"], rope_k, tk)
    o = _attn(q, k, v, tm, tk)
    x1, h2, comb = _merge(x, mod6, w["g1"], w["g2"], u, pool, o, w, tm)
    T = B * S
    tmoe = _pick(S, 1024) if mod6.shape[0] > 1 else _pick(T, 1024)
    x2 = _moe(h2.reshape(T, D), comb.reshape(T, LANES), x1.reshape(T, D), mod6,
              w["weg"], w["weu"], w["wed"], S, tmoe)
    return x2.reshape(B, S, D), ckv, krp


def kernel(x_prompt, x_sample, cache_ckv, cache_kr, c, c_ctx, norm1_g, norm2_g, w_ada, b_ada, w_in, conv_dw, conv_dw_b, conv_ln_g, conv_ln_b, w_conv_out, w_pool, pool_scale, w_pool_out, q_norm_g, w_uq, kv_norm_g, w_ukv, q_head_g, k_head_g, w_mla_out, w_out, w_router_g, b_router_g, w_router_e, b_router_e, w_exp_gate, w_exp_up, w_exp_down):
    p = dict(norm1_g=norm1_g, norm2_g=norm2_g, w_in=w_in, conv_dw=conv_dw, conv_dw_b=conv_dw_b,
             conv_ln_g=conv_ln_g, conv_ln_b=conv_ln_b, w_conv_out=w_conv_out, w_pool=w_pool,
             pool_scale=pool_scale, w_pool_out=w_pool_out, q_norm_g=q_norm_g, w_uq=w_uq,
             kv_norm_g=kv_norm_g, w_ukv=w_ukv, q_head_g=q_head_g, k_head_g=k_head_g,
             w_mla_out=w_mla_out, w_out=w_out, w_router_g=w_router_g, b_router_g=b_router_g,
             w_router_e=w_router_e, b_router_e=b_router_e, w_exp_gate=w_exp_gate,
             w_exp_up=w_exp_up, w_exp_down=w_exp_down)
    depth, d = norm1_g.shape
    db, ds, _ = x_sample.shape
    past = cache_ckv.shape[2]

    rows = -(-(1 + db) // 8) * 8
    cvec = jnp.zeros((rows, d), F32).at[0].set(c_ctx).at[1:1 + db].set(c)
    mod = _ada(cvec, w_ada, b_ada)

    rope_k = _rope_tables(ds, past)
    rope_q = tuple(t[:ds] for t in rope_k)
    cache_krp = jnp.pad(cache_kr, ((0, 0), (0, 0), (0, 0), (NOPE, LANES - NOPE - ROPE)))

    y_prompt, y_sample = x_prompt, x_sample
    new_ckv, new_kr = [], []
    for l in range(depth):
        w = _layer_weights(l, p)
        mod_ctx = mod[l, 0:1].reshape(1, 6, d)
        mod_lat = mod[l, 1:1 + db].reshape(db, 6, d)
        y_prompt, ckv_l, krp_l = _trunk_layer(y_prompt, mod_ctx, w, None, None, None, None)
        new_ckv.append(ckv_l)
        new_kr.append(krp_l[:, :, NOPE:NOPE + ROPE])
        y_sample, _, _ = _trunk_layer(y_sample, mod_lat, w, cache_ckv[:, l], cache_krp[:, l],
                                      rope_q, rope_k)
    return (y_prompt, y_sample, jnp.stack(new_ckv, axis=1), jnp.stack(new_kr, axis=1))
```

```python
import functools
import math

import jax
import jax.numpy as jnp
from jax import lax
from jax.experimental import pallas as pl
from jax.experimental.pallas import tpu as pltpu

F32 = jnp.float32
BF16 = jnp.bfloat16

LANES = 128
SUBLANES = 8
HALO = 16
EPS = 1e-6
GRID_W = 64
CONV_DIM = 512
CONV_WIDTH = 31
POOL_DIM = 512
POOL_WINDOWS = (2, 4, 8, 16)
POOL_GROUP_DIM = 128
HEADS = 8
NOPE = 64
ROPE = 32
QK_DIM = NOPE + ROPE
V_DIM = 64
Q_RANK = 256
KV_RANK = 128
ROPE_BASE = 10000.0
N_GROUPS = 4
EXP_PER_GROUP = 8
N_EXPERTS = N_GROUPS * EXP_PER_GROUP
VMEM_LIMIT = 56 * 1024 * 1024
NEG_BIG = -1e30
Q_SCALE = (QK_DIM ** -0.5) * math.log2(math.e)


def _const_spec(shape):
    nd = len(shape)
    return pl.BlockSpec(shape, lambda *_: (0,) * nd, pipeline_mode=pl.Buffered(1))


def _rms(x, n):
    return lax.rsqrt(jnp.sum(x * x, axis=-1, keepdims=True) * (1.0 / n) + EPS)


def _rope(x, c, s1, s2):
    return x * c + pltpu.roll(x, 8, 1) * s1 + pltpu.roll(x, LANES - 8, 1) * s2


def _ada_kernel(c_ref, w_ref, b_ref, o_ref):
    c = c_ref[...]
    s = c * jax.nn.sigmoid(c)
    o_ref[0] = jnp.dot(s, w_ref[0], preferred_element_type=F32,
                       precision=lax.Precision.HIGHEST) + b_ref[0]


def _ada(cvec, w_ada, b_ada):
    L, D, D6 = w_ada.shape
    rows = cvec.shape[0]
    tn = 1536
    return pl.pallas_call(
        _ada_kernel,
        grid=(L, D6 // tn),
        in_specs=[pl.BlockSpec((rows, D), lambda l, j: (0, 0)),
                  pl.BlockSpec((1, D, tn), lambda l, j: (l, 0, j)),
                  pl.BlockSpec((1, 1, tn), lambda l, j: (l, 0, j))],
        out_specs=pl.BlockSpec((1, rows, tn), lambda l, j: (l, 0, j)),
        out_shape=jax.ShapeDtypeStruct((L, rows, D6), F32),
        compiler_params=pltpu.CompilerParams(
            dimension_semantics=("parallel", "parallel"), vmem_limit_bytes=VMEM_LIMIT),
        name="ada",
    )(cvec, w_ada, b_ada.reshape(L, 1, D6))


def _pre_kernel(*refs, use_rope):
    if use_rope:
        (x_ref, mod_ref, g1_ref, wa_ref, qng_ref, kvg_ref, wuq_ref, qgain_ref, hsum_ref, hbc_ref,
         ta_ref, tb_ref, u_ref, pool_ref, q_ref, ckv_ref, krp_ref) = refs
    else:
        (x_ref, mod_ref, g1_ref, wa_ref, qng_ref, kvg_ref, wuq_ref, qgain_ref, hsum_ref, hbc_ref,
         u_ref, pool_ref, q_ref, ckv_ref, krp_ref) = refs
    x = x_ref[0]
    d = x.shape[-1]
    shift1 = mod_ref[0, 0:1, :]
    scale1 = mod_ref[0, 1:2, :]
    h = x * _rms(x, d) * g1_ref[...] * (1.0 + scale1) + shift1
    proj = jnp.dot(h.astype(BF16), wa_ref[...], preferred_element_type=F32)
    a = proj[:, 0:CONV_DIM]
    b = proj[:, CONV_DIM:2 * CONV_DIM]
    u_ref[0] = a * jax.nn.sigmoid(b)
    o = 2 * CONV_DIM
    pool_ref[0] = proj[:, o:o + POOL_DIM]
    o += POOL_DIM
    qd = proj[:, o:o + Q_RANK]
    o += Q_RANK
    kvd = proj[:, o:o + KV_RANK]
    o += KV_RANK
    krp_ref[0] = proj[:, o:o + LANES]
    ckv_ref[0] = kvd * _rms(kvd, KV_RANK) * kvg_ref[...]
    qn = (qd * _rms(qd, Q_RANK) * qng_ref[...]).astype(BF16)
    q2 = jnp.dot(qn, wuq_ref[...], preferred_element_type=F32)
    hw = HEADS * LANES
    q = q2[:, 0:hw]
    ssq = jnp.dot((q * q).astype(BF16), hsum_ref[...], preferred_element_type=F32)
    r = lax.rsqrt(ssq * (1.0 / QK_DIM) + EPS) * Q_SCALE
    r_hi = r.astype(BF16)
    r_lo = (r - r_hi.astype(F32)).astype(BF16)
    rb = jnp.dot(jnp.concatenate([r_hi, r_lo], axis=-1), hbc_ref[...], preferred_element_type=F32)
    for hd in range(HEADS):
        cols = slice(hd * LANES, (hd + 1) * LANES)
        if use_rope:
            qg = q[:, cols] * ta_ref[...] + q2[:, hw + hd * LANES:hw + (hd + 1) * LANES] * tb_ref[...]
        else:
            qg = q[:, cols] * qgain_ref[...]
        q_ref[0, hd] = (qg * rb[:, cols]).astype(BF16)


def _pre(x, mod6, g1, wa, qng, kvg, wuq, qgain, hsum, hbc, rope_tabs, tm):
    B, S, D = x.shape
    Bm = mod6.shape[0]
    use_rope = rope_tabs is not None
    mod_idx = (lambda b, i: (b, 0, 0)) if Bm > 1 else (lambda b, i: (0, 0, 0))
    in_specs = [pl.BlockSpec((1, tm, D), lambda b, i: (b, i, 0)),
                pl.BlockSpec((1, 6, D), mod_idx),
                _const_spec((1, D)),
                _const_spec(wa.shape),
                _const_spec((1, Q_RANK)),
                _const_spec((1, KV_RANK)),
                _const_spec(wuq.shape),
                _const_spec((1, LANES)),
                _const_spec(hsum.shape),
                _const_spec(hbc.shape)]
    args = [x, mod6, g1, wa, qng, kvg, wuq, qgain, hsum, hbc]
    if use_rope:
        in_specs += [pl.BlockSpec((tm, LANES), lambda b, i: (i, 0))] * 2
        args += list(rope_tabs)

    def tok(width):
        return pl.BlockSpec((1, tm, width), lambda b, i: (b, i, 0))

    return pl.pallas_call(
        functools.partial(_pre_kernel, use_rope=use_rope),
        grid=(B, S // tm),
        in_specs=in_specs,
        out_specs=[tok(CONV_DIM), tok(POOL_DIM),
                   pl.BlockSpec((1, HEADS, tm, LANES), lambda b, i: (b, 0, i, 0)),
                   tok(KV_RANK), tok(LANES)],
        out_shape=[jax.ShapeDtypeStruct((B, S, CONV_DIM), F32),
                   jax.ShapeDtypeStruct((B, S, POOL_DIM), F32),
                   jax.ShapeDtypeStruct((B, HEADS, S, LANES), BF16),
                   jax.ShapeDtypeStruct((B, S, KV_RANK), F32),
                   jax.ShapeDtypeStruct((B, S, LANES), F32)],
        compiler_params=pltpu.CompilerParams(
            dimension_semantics=("parallel", "parallel"), vmem_limit_bytes=VMEM_LIMIT),
        name="pre",
    )(*args)


def _kv_kernel(*refs, use_rope):
    if use_rope:
        (ckv_ref, krp_ref, wuk_ref, wuv_ref, kgain_ref, c_ref, s1_ref, s2_ref,
         k_ref, v_ref) = refs
    else:
        ckv_ref, krp_ref, wuk_ref, wuv_ref, kgain_ref, k_ref, v_ref = refs
    ckv = ckv_ref[0].astype(BF16)
    kn = jnp.dot(ckv, wuk_ref[...], preferred_element_type=F32)
    v = jnp.dot(ckv, wuv_ref[...], preferred_element_type=F32).astype(BF16)
    for pr in range(HEADS // 2):
        v_ref[0, pr] = v[:, pr * LANES:(pr + 1) * LANES]
    krp = krp_ref[0]
    kgain = kgain_ref[...]
    ssr = jnp.sum(krp * krp, axis=-1, keepdims=True)
    krg = krp * kgain
    if use_rope:
        krg = _rope(krg, c_ref[...], s1_ref[...], s2_ref[...])
    for hd in range(HEADS):
        knh = kn[:, hd * LANES:(hd + 1) * LANES]
        ssq = jnp.sum(knh * knh, axis=-1, keepdims=True) + ssr
        r = lax.rsqrt(ssq * (1.0 / QK_DIM) + EPS)
        k_ref[0, hd] = ((knh * kgain + krg) * r).astype(BF16)


def _kv(ckv, krp, wuk, wuv, kgain, rope_tabs, tk):
    B, Sk, _ = ckv.shape
    use_rope = rope_tabs is not None
    in_specs = [pl.BlockSpec((1, tk, KV_RANK), lambda b, i: (b, i, 0)),
                pl.BlockSpec((1, tk, LANES), lambda b, i: (b, i, 0)),
                _const_spec(wuk.shape), _const_spec(wuv.shape), _const_spec((1, LANES))]
    args = [ckv, krp, wuk, wuv, kgain]
    if use_rope:
        in_specs += [pl.BlockSpec((tk, LANES), lambda b, i: (i, 0))] * 3
        args += list(rope_tabs)
    return pl.pallas_call(
        functools.partial(_kv_kernel, use_rope=use_rope),
        grid=(B, Sk // tk),
        in_specs=in_specs,
        out_specs=[pl.BlockSpec((1, HEADS, tk, LANES), lambda b, i: (b, 0, i, 0)),
                   pl.BlockSpec((1, HEADS // 2, tk, LANES), lambda b, i: (b, 0, i, 0))],
        out_shape=[jax.ShapeDtypeStruct((B, HEADS, Sk, LANES), BF16),
                   jax.ShapeDtypeStruct((B, HEADS // 2, Sk, LANES), BF16)],
        compiler_params=pltpu.CompilerParams(
            dimension_semantics=("parallel", "parallel"), vmem_limit_bytes=VMEM_LIMIT),
        name="kv",
    )(*args)


def _attn_kernel(q_ref, k_ref, v_ref, o_ref, s_ref, p_ref, *, chunks):
    tq = q_ref.shape[2]
    lane = lax.broadcasted_iota(jnp.int32, (tq, LANES), 1)
    outs = []
    for sub in range(2):
        qh = q_ref[0, sub]
        m_part = jnp.full((tq, LANES), NEG_BIG, F32)
        for c0, cw in chunks:
            s = lax.dot_general(qh, k_ref[0, sub, c0:c0 + cw, :], (((1,), (1,)), ((), ())),
                                preferred_element_type=F32)
            s_ref[sub, :, c0:c0 + cw] = s
            for j in range(cw // LANES):
                m_part = jnp.maximum(m_part, s[:, j * LANES:(j + 1) * LANES])
        m = jnp.max(m_part, axis=-1, keepdims=True)
        l_part = jnp.zeros((tq, LANES), F32)
        for c0, cw in chunks:
            p = jnp.exp2(s_ref[sub, :, c0:c0 + cw] - m)
            for j in range(cw // LANES):
                l_part = l_part + p[:, j * LANES:(j + 1) * LANES]
            p_ref[sub, :, c0:c0 + cw] = p.astype(BF16)
        acc = jnp.dot(p_ref[sub], v_ref[0, 0], preferred_element_type=F32)
        outs.append(acc / jnp.sum(l_part, axis=-1, keepdims=True))
    o_ref[0] = jnp.where(lane < V_DIM, outs[0], outs[1]).astype(BF16)


def _attn(q, k, v, tq, tk):
    B, H, S, _ = q.shape
    Sk = k.shape[2]
    chunks = tuple((c0, min(tk, Sk - c0)) for c0 in range(0, Sk, tk))
    return pl.pallas_call(
        functools.partial(_attn_kernel, chunks=chunks),
        grid=(B, H // 2, S // tq),
        in_specs=[pl.BlockSpec((1, 2, tq, LANES), lambda b, p, i: (b, p, i, 0)),
                  pl.BlockSpec((1, 2, Sk, LANES), lambda b, p, i: (b, p, 0, 0)),
                  pl.BlockSpec((1, 1, Sk, LANES), lambda b, p, i: (b, p, 0, 0))],
        out_specs=pl.BlockSpec((1, tq, LANES), lambda b, p, i: (b, i, p)),
        out_shape=jax.ShapeDtypeStruct((B, S, H * V_DIM), BF16),
        scratch_shapes=[pltpu.VMEM((2, tq, Sk), F32), pltpu.VMEM((2, tq, Sk), BF16)],
        compiler_params=pltpu.CompilerParams(
            dimension_semantics=("parallel", "parallel", "arbitrary"),
            vmem_limit_bytes=VMEM_LIMIT),
        name="attn",
    )(q, k, v)


def _merge_kernel(x_ref, mod_ref, g1_ref, g2_ref,
                  u_ref, up_ref, un_ref, p_ref, pp_ref, pn_ref, o_ref,
                  wg_ref, dw_ref, dwb_ref, lng_ref, lnb_ref, wco_ref,
                  wpool_ref, pscale_ref, wpo_ref, wmo_ref, wout_ref,
                  wr1_ref, wr2_ref, br_ref,
                  x1_ref, h2_ref, comb_ref,
                  ubuf, pbuf, ybuf, mbuf, *, seq_len):
    i = pl.program_id(1)
    n_i = pl.num_programs(1)
    tm = x_ref.shape[1]
    d = x_ref.shape[2]
    x = x_ref[0]
    shift1 = mod_ref[0, 0:1, :]
    scale1 = mod_ref[0, 1:2, :]
    gate1 = mod_ref[0, 2:3, :]
    shift2 = mod_ref[0, 3:4, :]
    scale2 = mod_ref[0, 4:5, :]
    h = (x * _rms(x, d) * g1_ref[...] * (1.0 + scale1) + shift1).astype(BF16)
    gates = jax.nn.sigmoid(jnp.dot(h, wg_ref[...], preferred_element_type=F32))

    has_prev = (i > 0).astype(F32)
    has_next = (i < n_i - 1).astype(F32)
    ubuf[0, 0:HALO, :] = up_ref[0] * has_prev
    ubuf[0, HALO:HALO + tm, :] = u_ref[0]
    ubuf[0, HALO + tm:, :] = un_ref[0] * has_next
    pbuf[0:HALO, :] = pp_ref[0] * has_prev
    pbuf[HALO:HALO + tm, :] = p_ref[0]
    pbuf[HALO + tm:, :] = pn_ref[0] * has_next
    n_sh = tm + 2 * HALO - SUBLANES
    for sft in range(1, SUBLANES):
        ubuf[sft, 0:n_sh, :] = ubuf[0, sft:sft + n_sh, :]

    rc = 32
    half = CONV_WIDTH // 2
    for r0 in range(0, tm, rc):
        acc = jnp.zeros((rc, CONV_DIM), F32) + dwb_ref[...]
        for k in range(CONV_WIDTH):
            off = HALO - half + k
            row = r0 + (off // SUBLANES) * SUBLANES
            acc = acc + dw_ref[k:k + 1, :] * ubuf[off % SUBLANES, row:row + rc, :]
        mu = jnp.mean(acc, axis=-1, keepdims=True)
        cen = acc - mu
        var = jnp.mean(cen * cen, axis=-1, keepdims=True)
        yn = cen * lax.rsqrt(var + EPS) * lng_ref[...] + lnb_ref[...]
        ybuf[r0:r0 + rc, :] = (yn * jax.nn.sigmoid(yn)).astype(BF16)
    y_conv = jnp.dot(ybuf[...], wco_ref[...], preferred_element_type=F32)

    t = i * tm + lax.broadcasted_iota(jnp.int32, (tm, 1), 0)
    for g, w in enumerate(POOL_WINDOWS):
        cols = slice(g * POOL_GROUP_DIM, (g + 1) * POOL_GROUP_DIM)
        s = pbuf[HALO - w // 2:HALO - w // 2 + tm, cols]
        for j in range(1, w):
            s = s + pbuf[HALO - w // 2 + j:HALO - w // 2 + j + tm, cols]
        cnt = (jnp.minimum(t + w // 2, seq_len) - jnp.maximum(t - w // 2, 0)).astype(F32)
        mbuf[:, cols] = (s / cnt - pbuf[HALO:HALO + tm, cols]).astype(BF16)
    y_pool = jnp.dot(mbuf[...], wpool_ref[...], preferred_element_type=F32) * pscale_ref[...]
    y_pool = jnp.dot(y_pool.astype(BF16), wpo_ref[...], preferred_element_type=F32)

    y_mla = jnp.dot(o_ref[0], wmo_ref[...], preferred_element_type=F32)

    mixed = (gates[:, 0:d] * y_conv + gates[:, d:2 * d] * y_pool + gates[:, 2 * d:3 * d] * y_mla)
    x1 = x + gate1 * jnp.dot(mixed.astype(BF16), wout_ref[...], preferred_element_type=F32)
    x1_ref[0] = x1

    h2 = x1 * _rms(x1, d) * g2_ref[...] * (1.0 + scale2) + shift2
    hi = h2.astype(BF16)
    h2_ref[0] = hi
    lo = (h2 - hi.astype(F32)).astype(BF16)
    r1 = jnp.dot(hi, wr1_ref[...], preferred_element_type=F32)
    r2 = jnp.dot(lo, wr2_ref[...], preferred_element_type=F32)
    logits = r1[:, 0:LANES] + r1[:, LANES:2 * LANES] + r2 + br_ref[...]

    lane = lax.broadcasted_iota(jnp.int32, (tm, LANES), 1)
    is_g = (lane >= N_EXPERTS) & (lane < N_EXPERTS + N_GROUPS)
    lg = jnp.where(is_g, logits, NEG_BIG)
    mg = jnp.max(lg, axis=-1, keepdims=True)
    eg = jnp.where(is_g, jnp.exp(lg - mg), 0.0)
    pg = eg / jnp.sum(eg, axis=-1, keepdims=True)
    pg_max = jnp.max(pg, axis=-1, keepdims=True)
    g_idx = jnp.min(jnp.where(is_g & (pg == pg_max), lane, 4 * LANES), axis=-1,
                    keepdims=True) - N_EXPERTS
    in_grp = (lane < N_EXPERTS) & ((lane >> 3) == g_idx)
    le = jnp.where(in_grp, logits, NEG_BIG)
    me = jnp.max(le, axis=-1, keepdims=True)
    ee = jnp.where(in_grp, jnp.exp(le - me), 0.0)
    pe = ee / jnp.sum(ee, axis=-1, keepdims=True)
    pe_m = jnp.where(in_grp, pe, -1.0)
    p1 = jnp.max(pe_m, axis=-1, keepdims=True)
    i1 = jnp.min(jnp.where(pe_m == p1, lane, 4 * LANES), axis=-1, keepdims=True)
    pe_m2 = jnp.where(lane == i1, -1.0, pe_m)
    p2 = jnp.max(pe_m2, axis=-1, keepdims=True)
    i2 = jnp.min(jnp.where(pe_m2 == p2, lane, 4 * LANES), axis=-1, keepdims=True)
    tot = p1 + p2
    comb_ref[0] = jnp.where(lane == i1, pg_max * (p1 / tot),
                            jnp.where(lane == i2, pg_max * (p2 / tot), 0.0))


def _merge(x, mod6, g1, g2, u, pool, o, w, tm):
    B, S, D = x.shape
    Bm = mod6.shape[0]
    nh = tm // HALO
    last_h = S // HALO - 1
    mod_idx = (lambda b, i: (b, 0, 0)) if Bm > 1 else (lambda b, i: (0, 0, 0))

    def tok(width):
        return pl.BlockSpec((1, tm, width), lambda b, i: (b, i, 0))

    def prev(width):
        return pl.BlockSpec((1, HALO, width), lambda b, i: (b, jnp.maximum(i * nh - 1, 0), 0))

    def nxt(width):
        return pl.BlockSpec((1, HALO, width), lambda b, i: (b, jnp.minimum((i + 1) * nh, last_h), 0))

    weights = [w["wg"], w["dw"], w["dwb"], w["lng"], w["lnb"], w["wco"], w["wpool"], w["pscale"],
               w["wpo"], w["wmo"], w["wout"], w["wr1"], w["wr2"], w["br"]]
    in_specs = ([tok(D), pl.BlockSpec((1, 6, D), mod_idx), _const_spec((1, D)), _const_spec((1, D)),
                 tok(CONV_DIM), prev(CONV_DIM), nxt(CONV_DIM),
                 tok(POOL_DIM), prev(POOL_DIM), nxt(POOL_DIM), tok(HEADS * V_DIM)]
                + [_const_spec(a.shape) for a in weights])
    return pl.pallas_call(
        functools.partial(_merge_kernel, seq_len=S),
        grid=(B, S // tm),
        in_specs=in_specs,
        out_specs=[tok(D), tok(D), tok(LANES)],
        out_shape=[jax.ShapeDtypeStruct((B, S, D), F32),
                   jax.ShapeDtypeStruct((B, S, D), BF16),
                   jax.ShapeDtypeStruct((B, S, LANES), F32)],
        scratch_shapes=[pltpu.VMEM((SUBLANES, tm + 2 * HALO, CONV_DIM), F32),
                        pltpu.VMEM((tm + 2 * HALO, POOL_DIM), F32),
                        pltpu.VMEM((tm, CONV_DIM), BF16),
                        pltpu.VMEM((tm, POOL_DIM), BF16)],
        compiler_params=pltpu.CompilerParams(
            dimension_semantics=("parallel", "arbitrary"), vmem_limit_bytes=VMEM_LIMIT),
        name="merge",
    )(x, mod6, g1, g2, u, u, u, pool, pool, pool, o, *weights)


def _moe_kernel(h2_ref, comb_ref, x1_ref, mod_ref, wg_ref, wu_ref, wd_ref, o_ref, acc_ref):
    e = pl.program_id(1)

    @pl.when(e == 0)
    def _():
        acc_ref[...] = jnp.zeros_like(acc_ref)

    hb = h2_ref[...]
    a = jnp.dot(hb, wg_ref[0], preferred_element_type=F32)
    u = jnp.dot(hb, wu_ref[0], preferred_element_type=F32)
    lane = lax.broadcasted_iota(jnp.int32, comb_ref.shape, 1)
    c = jnp.sum(jnp.where(lane == e, comb_ref[...], 0.0), axis=-1, keepdims=True)
    hid = (a * jax.nn.sigmoid(a)) * u * c
    acc_ref[...] += jnp.dot(hid.astype(BF16), wd_ref[0], preferred_element_type=F32)

    @pl.when(e == pl.num_programs(1) - 1)
    def _():
        o_ref[...] = x1_ref[...] + mod_ref[0, 5:6, :] * acc_ref[...]


def _moe(h2, comb, x1, mod6, wg, wu, wd, seq_len, tm):
    T, D = x1.shape
    E, _, F = wg.shape
    Bm = mod6.shape[0]
    per_seq = seq_len // tm
    mod_idx = (lambda i, e: (i // per_seq, 0, 0)) if Bm > 1 else (lambda i, e: (0, 0, 0))
    return pl.pallas_call(
        _moe_kernel,
        grid=(T // tm, E),
        in_specs=[pl.BlockSpec((tm, D), lambda i, e: (i, 0)),
                  pl.BlockSpec((tm, LANES), lambda i, e: (i, 0)),
                  pl.BlockSpec((tm, D), lambda i, e: (i, 0)),
                  pl.BlockSpec((1, 6, D), mod_idx),
                  pl.BlockSpec((1, D, F), lambda i, e: (e, 0, 0)),
                  pl.BlockSpec((1, D, F), lambda i, e: (e, 0, 0)),
                  pl.BlockSpec((1, F, D), lambda i, e: (e, 0, 0))],
        out_specs=pl.BlockSpec((tm, D), lambda i, e: (i, 0)),
        out_shape=jax.ShapeDtypeStruct((T, D), F32),
        scratch_shapes=[pltpu.VMEM((tm, D), F32)],
        compiler_params=pltpu.CompilerParams(
            dimension_semantics=("parallel", "arbitrary"), vmem_limit_bytes=VMEM_LIMIT),
        name="moe",
    )(h2, comb, x1, mod6, wg, wu, wd)


def _rope_tables(seq_len, n_ctx):
    rows = seq_len // GRID_W
    row = jnp.repeat(jnp.arange(rows), GRID_W).astype(F32)
    col = jnp.tile(jnp.arange(GRID_W), rows).astype(F32)
    half = ROPE // 2
    inv = 1.0 / (ROPE_BASE ** (jnp.arange(0, half, 2, dtype=F32) / half))
    ang = jnp.concatenate([row[:, None] * inv, col[:, None] * inv], axis=-1)
    cos, sin = jnp.cos(ang), jnp.sin(ang)
    q4 = ROPE // 4
    zero = jnp.zeros((seq_len, q4), F32)
    c_parts, s1_parts, s2_parts = [jnp.ones((seq_len, NOPE), F32)], [jnp.zeros((seq_len, NOPE), F32)], \
        [jnp.zeros((seq_len, NOPE), F32)]
    for hf in range(2):
        cs, sn = cos[:, hf * q4:(hf + 1) * q4], sin[:, hf * q4:(hf + 1) * q4]
        c_parts += [cs, cs]
        s1_parts += [zero, sn]
        s2_parts += [-sn, zero]
    pad = LANES - NOPE - ROPE
    c_parts.append(jnp.ones((seq_len, pad), F32))
    s1_parts.append(jnp.zeros((seq_len, pad), F32))
    s2_parts.append(jnp.zeros((seq_len, pad), F32))
    c, s1, s2 = (jnp.concatenate(p, axis=-1) for p in (c_parts, s1_parts, s2_parts))
    if n_ctx:
        c = jnp.concatenate([c, jnp.ones((n_ctx, LANES), F32)], axis=0)
        s1 = jnp.concatenate([s1, jnp.zeros((n_ctx, LANES), F32)], axis=0)
        s2 = jnp.concatenate([s2, jnp.zeros((n_ctx, LANES), F32)], axis=0)
    return c, s1, s2


def _rope_partner():
    lane = jnp.arange(LANES)
    q4 = ROPE // 4
    in_rope = (lane >= NOPE) & (lane < NOPE + ROPE)
    first = ((lane - NOPE) // q4) % 2 == 0
    partner = jnp.where(in_rope, jnp.where(first, lane + q4, lane - q4), lane)
    return partner, in_rope.astype(F32)


def _head_matrices():
    row_head = jnp.arange(HEADS * LANES) // LANES
    hsum = (row_head[:, None] == jnp.arange(LANES)[None, :]).astype(BF16)
    hbc = jnp.concatenate([hsum.T, hsum.T], axis=0)
    return hsum, hbc


def _layer_weights(l, p):
    d = p["w_in"].shape[1]
    w_in = p["w_in"][l]
    n_a = 2 * CONV_DIM + POOL_DIM + Q_RANK + KV_RANK
    wa = jnp.concatenate([w_in[:, :n_a], jnp.zeros((d, NOPE), F32), w_in[:, n_a:n_a + ROPE],
                          jnp.zeros((d, LANES - NOPE - ROPE), F32)], axis=1).astype(BF16)
    wg = w_in[:, n_a + ROPE:].astype(BF16)
    wuq3 = jnp.pad(p["w_uq"][l].reshape(Q_RANK, HEADS, QK_DIM), ((0, 0), (0, 0), (0, LANES - QK_DIM)))
    wuq = wuq3.reshape(Q_RANK, HEADS * LANES).astype(BF16)
    partner, is_rope = _rope_partner()
    wuq_sw = (wuq3[:, :, partner] * is_rope).reshape(Q_RANK, HEADS * LANES).astype(BF16)
    qgain = jnp.pad(p["q_head_g"][l], (0, LANES - QK_DIM))
    wukv = p["w_ukv"][l].reshape(KV_RANK, HEADS, NOPE + V_DIM)
    wuk = jnp.pad(wukv[:, :, :NOPE], ((0, 0), (0, 0), (0, LANES - NOPE))).reshape(
        KV_RANK, HEADS * LANES).astype(BF16)
    wuv = wukv[:, :, NOPE:].reshape(KV_RANK, HEADS * V_DIM).astype(BF16)
    pad_gain = lambda g: jnp.pad(g, (0, LANES - QK_DIM)).reshape(1, LANES)
    wpool = jnp.zeros((POOL_DIM, POOL_DIM), F32)
    for g in range(len(POOL_WINDOWS)):
        sl = slice(g * POOL_GROUP_DIM, (g + 1) * POOL_GROUP_DIM)
        wpool = wpool.at[sl, sl].set(p["w_pool"][l, g])
    wr = jnp.concatenate([p["w_router_e"][l], p["w_router_g"][l]], axis=1)
    wr = jnp.pad(wr, ((0, 0), (0, LANES - wr.shape[1])))
    wr_hi = wr.astype(BF16)
    wr_lo = (wr - wr_hi.astype(F32)).astype(BF16)
    br = jnp.concatenate([p["b_router_e"][l], p["b_router_g"][l]])
    br = jnp.pad(br, (0, LANES - br.shape[0])).reshape(1, LANES)
    return dict(
        g1=p["norm1_g"][l].reshape(1, d), g2=p["norm2_g"][l].reshape(1, d),
        wa=wa, wg=wg, wuq=wuq, wuq_rope=jnp.concatenate([wuq, wuq_sw], axis=1),
        qgain_partner=qgain[partner].reshape(1, LANES), wuk=wuk, wuv=wuv,
        qng=p["q_norm_g"][l].reshape(1, Q_RANK), kvg=p["kv_norm_g"][l].reshape(1, KV_RANK),
        qgain=pad_gain(p["q_head_g"][l]), kgain=pad_gain(p["k_head_g"][l]),
        dw=p["conv_dw"][l], dwb=p["conv_dw_b"][l].reshape(1, CONV_DIM),
        lng=p["conv_ln_g"][l].reshape(1, CONV_DIM), lnb=p["conv_ln_b"][l].reshape(1, CONV_DIM),
        wco=p["w_conv_out"][l].astype(BF16), wpool=wpool.astype(BF16),
        pscale=p["pool_scale"][l].reshape(1, POOL_DIM), wpo=p["w_pool_out"][l].astype(BF16),
        wmo=p["w_mla_out"][l].astype(BF16), wout=p["w_out"][l].astype(BF16),
        wr1=jnp.concatenate([wr_hi, wr_lo], axis=1), wr2=wr_hi, br=br,
        weg=p["w_exp_gate"][l].astype(BF16), weu=p["w_exp_up"][l].astype(BF16),
        wed=p["w_exp_down"][l].astype(BF16))


def _pick(n, pref):
    t = min(n, pref)
    while n % t:
        t //= 2
    return t


def _trunk_layer(x, mod6, w, ctx_ckv, ctx_krp, rope_k, head_mats):
    B, S, D = x.shape
    tm = _pick(S, 512)
    if rope_k is not None:
        c, s1, s2 = (t[:S] for t in rope_k)
        rope_q = (c * w["qgain"], (s1 + s2) * w["qgain_partner"])
        wuq = w["wuq_rope"]
    else:
        rope_q, wuq = None, w["wuq"]
    u, pool, q, ckv, krp = _pre(x, mod6, w["g1"], w["wa"], w["qng"], w["kvg"], wuq, w["qgain"],
                                *head_mats, rope_q, tm)
    if ctx_ckv is not None:
        ckv_all = jnp.concatenate([ckv, ctx_ckv], axis=1)
        krp_all = jnp.concatenate([krp, ctx_krp], axis=1)
    else:
        ckv_all, krp_all = ckv, krp
    Sk = ckv_all.shape[1]
    k, v = _kv(ckv_all, krp_all, w["wuk"], w["wuv"], w["kgain"], rope_k, _pick(Sk, 256))
    o = _attn(q, k, v, tm, 512)
    x1, h2, comb = _merge(x, mod6, w["g1"], w["g2"], u, pool, o, w, _pick(S, 256))
    T = B * S
    tmoe = _pick(S, 1024) if mod6.shape[0] > 1 else _pick(T, 1024)
    x2 = _moe(h2.reshape(T, D), comb.reshape(T, LANES), x1.reshape(T, D), mod6,
              w["weg"], w["weu"], w["wed"], S, tmoe)
    return x2.reshape(B, S, D), ckv, krp


def kernel(x_prompt, x_sample, cache_ckv, cache_kr, c, c_ctx, norm1_g, norm2_g, w_ada, b_ada, w_in, conv_dw, conv_dw_b, conv_ln_g, conv_ln_b, w_conv_out, w_pool, pool_scale, w_pool_out, q_norm_g, w_uq, kv_norm_g, w_ukv, q_head_g, k_head_g, w_mla_out, w_out, w_router_g, b_router_g, w_router_e, b_router_e, w_exp_gate, w_exp_up, w_exp_down):
    p = dict(norm1_g=norm1_g, norm2_g=norm2_g, w_in=w_in, conv_dw=conv_dw, conv_dw_b=conv_dw_b,
             conv_ln_g=conv_ln_g, conv_ln_b=conv_ln_b, w_conv_out=w_conv_out, w_pool=w_pool,
             pool_scale=pool_scale, w_pool_out=w_pool_out, q_norm_g=q_norm_g, w_uq=w_uq,
             kv_norm_g=kv_norm_g, w_ukv=w_ukv, q_head_g=q_head_g, k_head_g=k_head_g,
             w_mla_out=w_mla_out, w_out=w_out, w_router_g=w_router_g, b_router_g=b_router_g,
             w_router_e=w_router_e, b_router_e=b_router_e, w_exp_gate=w_exp_gate,
             w_exp_up=w_exp_up, w_exp_down=w_exp_down)
    depth, d = norm1_g.shape
    db, ds, _ = x_sample.shape
    past = cache_ckv.shape[2]

    rows = -(-(1 + db) // 8) * 8
    cvec = jnp.zeros((rows, d), F32).at[0].set(c_ctx).at[1:1 + db].set(c)
    mod = _ada(cvec, w_ada, b_ada)

    rope_k = _rope_tables(ds, past)
    head_mats = _head_matrices()
    cache_krp = jnp.pad(cache_kr, ((0, 0), (0, 0), (0, 0), (NOPE, LANES - NOPE - ROPE)))

    y_prompt, y_sample = x_prompt, x_sample
    new_ckv, new_kr = [], []
    for l in range(depth):
        w = _layer_weights(l, p)
        mod_ctx = mod[l, 0:1].reshape(1, 6, d)
        mod_lat = mod[l, 1:1 + db].reshape(db, 6, d)
        y_prompt, ckv_l, krp_l = _trunk_layer(y_prompt, mod_ctx, w, None, None, None, head_mats)
        new_ckv.append(ckv_l)
        new_kr.append(krp_l[:, :, NOPE:NOPE + ROPE])
        y_sample, _, _ = _trunk_layer(y_sample, mod_lat, w, cache_ckv[:, l], cache_krp[:, l],
                                      rope_k, head_mats)
    return (y_prompt, y_sample, jnp.stack(new_ckv, axis=1), jnp.stack(new_kr, axis=1))
```

```python
import functools
import math

import jax
import jax.numpy as jnp
from jax import lax
from jax.experimental import pallas as pl
from jax.experimental.pallas import tpu as pltpu

F32 = jnp.float32
BF16 = jnp.bfloat16

LANES = 128
SUBLANES = 8
HALO = 16
EPS = 1e-6
GRID_W = 64
CONV_DIM = 512
CONV_WIDTH = 31
POOL_DIM = 512
POOL_WINDOWS = (2, 4, 8, 16)
POOL_GROUP_DIM = 128
HEADS = 8
NOPE = 64
ROPE = 32
QK_DIM = NOPE + ROPE
V_DIM = 64
Q_RANK = 256
KV_RANK = 128
ROPE_BASE = 10000.0
N_GROUPS = 4
EXP_PER_GROUP = 8
N_EXPERTS = N_GROUPS * EXP_PER_GROUP
N_CLASSES = N_GROUPS * EXP_PER_GROUP * EXP_PER_GROUP
N_USED_CLASSES = N_GROUPS * EXP_PER_GROUP * (EXP_PER_GROUP - 1) // 2
ROW_TILE = 128
VMEM_LIMIT = 56 * 1024 * 1024
NEG_BIG = -1e30
Q_SCALE = (QK_DIM ** -0.5) * math.log2(math.e)


def _const_spec(shape):
    nd = len(shape)
    return pl.BlockSpec(shape, lambda *_: (0,) * nd, pipeline_mode=pl.Buffered(1))


def _rms(x, n):
    return lax.rsqrt(jnp.sum(x * x, axis=-1, keepdims=True) * (1.0 / n) + EPS)


def _rope(x, c, s1, s2):
    return x * c + pltpu.roll(x, 8, 1) * s1 + pltpu.roll(x, LANES - 8, 1) * s2


def _ada_kernel(c_ref, w_ref, b_ref, o_ref):
    c = c_ref[...]
    s = c * jax.nn.sigmoid(c)
    o_ref[0] = jnp.dot(s, w_ref[0], preferred_element_type=F32,
                       precision=lax.Precision.HIGHEST) + b_ref[0]


def _ada(cvec, w_ada, b_ada):
    L, D, D6 = w_ada.shape
    rows = cvec.shape[0]
    tn = 1536
    return pl.pallas_call(
        _ada_kernel,
        grid=(L, D6 // tn),
        in_specs=[pl.BlockSpec((rows, D), lambda l, j: (0, 0)),
                  pl.BlockSpec((1, D, tn), lambda l, j: (l, 0, j)),
                  pl.BlockSpec((1, 1, tn), lambda l, j: (l, 0, j))],
        out_specs=pl.BlockSpec((1, rows, tn), lambda l, j: (l, 0, j)),
        out_shape=jax.ShapeDtypeStruct((L, rows, D6), F32),
        compiler_params=pltpu.CompilerParams(
            dimension_semantics=("parallel", "parallel"), vmem_limit_bytes=VMEM_LIMIT),
        name="ada",
    )(cvec, w_ada, b_ada.reshape(L, 1, D6))


def _pre_kernel(*refs, use_rope):
    if use_rope:
        (x_ref, mod_ref, g1_ref, wa_ref, qng_ref, kvg_ref, wuq_ref, qgain_ref, hsum_ref, hbc_ref,
         ta_ref, tb_ref, u_ref, pool_ref, q_ref, ckv_ref, krp_ref) = refs
    else:
        (x_ref, mod_ref, g1_ref, wa_ref, qng_ref, kvg_ref, wuq_ref, qgain_ref, hsum_ref, hbc_ref,
         u_ref, pool_ref, q_ref, ckv_ref, krp_ref) = refs
    x = x_ref[0]
    d = x.shape[-1]
    shift1 = mod_ref[0, 0:1, :]
    scale1 = mod_ref[0, 1:2, :]
    h = x * _rms(x, d) * g1_ref[...] * (1.0 + scale1) + shift1
    proj = jnp.dot(h.astype(BF16), wa_ref[...], preferred_element_type=F32)
    a = proj[:, 0:CONV_DIM]
    b = proj[:, CONV_DIM:2 * CONV_DIM]
    u_ref[0] = a * jax.nn.sigmoid(b)
    o = 2 * CONV_DIM
    pool_ref[0] = proj[:, o:o + POOL_DIM]
    o += POOL_DIM
    qd = proj[:, o:o + Q_RANK]
    o += Q_RANK
    kvd = proj[:, o:o + KV_RANK]
    o += KV_RANK
    krp_ref[0] = proj[:, o:o + LANES]
    ckv_ref[0] = kvd * _rms(kvd, KV_RANK) * kvg_ref[...]
    qn = (qd * _rms(qd, Q_RANK) * qng_ref[...]).astype(BF16)
    q2 = jnp.dot(qn, wuq_ref[...], preferred_element_type=F32)
    hw = HEADS * LANES
    q = q2[:, 0:hw]
    ssq = jnp.dot((q * q).astype(BF16), hsum_ref[...], preferred_element_type=F32)
    r = lax.rsqrt(ssq * (1.0 / QK_DIM) + EPS) * Q_SCALE
    r_hi = r.astype(BF16)
    r_lo = (r - r_hi.astype(F32)).astype(BF16)
    rb = jnp.dot(jnp.concatenate([r_hi, r_lo], axis=-1), hbc_ref[...], preferred_element_type=F32)
    for hd in range(HEADS):
        cols = slice(hd * LANES, (hd + 1) * LANES)
        if use_rope:
            qg = q[:, cols] * ta_ref[...] + q2[:, hw + hd * LANES:hw + (hd + 1) * LANES] * tb_ref[...]
        else:
            qg = q[:, cols] * qgain_ref[...]
        q_ref[0, hd] = (qg * rb[:, cols]).astype(BF16)


def _pre(x, mod6, g1, wa, qng, kvg, wuq, qgain, hsum, hbc, rope_tabs, tm):
    B, S, D = x.shape
    Bm = mod6.shape[0]
    use_rope = rope_tabs is not None
    mod_idx = (lambda b, i: (b, 0, 0)) if Bm > 1 else (lambda b, i: (0, 0, 0))
    in_specs = [pl.BlockSpec((1, tm, D), lambda b, i: (b, i, 0)),
                pl.BlockSpec((1, 6, D), mod_idx),
                _const_spec((1, D)),
                _const_spec(wa.shape),
                _const_spec((1, Q_RANK)),
                _const_spec((1, KV_RANK)),
                _const_spec(wuq.shape),
                _const_spec((1, LANES)),
                _const_spec(hsum.shape),
                _const_spec(hbc.shape)]
    args = [x, mod6, g1, wa, qng, kvg, wuq, qgain, hsum, hbc]
    if use_rope:
        in_specs += [pl.BlockSpec((tm, LANES), lambda b, i: (i, 0))] * 2
        args += list(rope_tabs)

    def tok(width):
        return pl.BlockSpec((1, tm, width), lambda b, i: (b, i, 0))

    return pl.pallas_call(
        functools.partial(_pre_kernel, use_rope=use_rope),
        grid=(B, S // tm),
        in_specs=in_specs,
        out_specs=[tok(CONV_DIM), tok(POOL_DIM),
                   pl.BlockSpec((1, HEADS, tm, LANES), lambda b, i: (b, 0, i, 0)),
                   tok(KV_RANK), tok(LANES)],
        out_shape=[jax.ShapeDtypeStruct((B, S, CONV_DIM), F32),
                   jax.ShapeDtypeStruct((B, S, POOL_DIM), F32),
                   jax.ShapeDtypeStruct((B, HEADS, S, LANES), BF16),
                   jax.ShapeDtypeStruct((B, S, KV_RANK), F32),
                   jax.ShapeDtypeStruct((B, S, LANES), F32)],
        compiler_params=pltpu.CompilerParams(
            dimension_semantics=("parallel", "parallel"), vmem_limit_bytes=VMEM_LIMIT),
        name="pre",
    )(*args)


def _kv_kernel(*refs, use_rope):
    if use_rope:
        (ckv_ref, krp_ref, wuk_ref, wuv_ref, kgain_ref, c_ref, s1_ref, s2_ref,
         k_ref, v_ref) = refs
    else:
        ckv_ref, krp_ref, wuk_ref, wuv_ref, kgain_ref, k_ref, v_ref = refs
    ckv = ckv_ref[0].astype(BF16)
    kn = jnp.dot(ckv, wuk_ref[...], preferred_element_type=F32)
    v = jnp.dot(ckv, wuv_ref[...], preferred_element_type=F32).astype(BF16)
    for pr in range(HEADS // 2):
        v_ref[0, pr] = v[:, pr * LANES:(pr + 1) * LANES]
    krp = krp_ref[0]
    kgain = kgain_ref[...]
    ssr = jnp.sum(krp * krp, axis=-1, keepdims=True)
    krg = krp * kgain
    if use_rope:
        krg = _rope(krg, c_ref[...], s1_ref[...], s2_ref[...])
    for hd in range(HEADS):
        knh = kn[:, hd * LANES:(hd + 1) * LANES]
        ssq = jnp.sum(knh * knh, axis=-1, keepdims=True) + ssr
        r = lax.rsqrt(ssq * (1.0 / QK_DIM) + EPS)
        k_ref[0, hd] = ((knh * kgain + krg) * r).astype(BF16)


def _kv(ckv, krp, wuk, wuv, kgain, rope_tabs, tk):
    B, Sk, _ = ckv.shape
    use_rope = rope_tabs is not None
    in_specs = [pl.BlockSpec((1, tk, KV_RANK), lambda b, i: (b, i, 0)),
                pl.BlockSpec((1, tk, LANES), lambda b, i: (b, i, 0)),
                _const_spec(wuk.shape), _const_spec(wuv.shape), _const_spec((1, LANES))]
    args = [ckv, krp, wuk, wuv, kgain]
    if use_rope:
        in_specs += [pl.BlockSpec((tk, LANES), lambda b, i: (i, 0))] * 3
        args += list(rope_tabs)
    return pl.pallas_call(
        functools.partial(_kv_kernel, use_rope=use_rope),
        grid=(B, Sk // tk),
        in_specs=in_specs,
        out_specs=[pl.BlockSpec((1, HEADS, tk, LANES), lambda b, i: (b, 0, i, 0)),
                   pl.BlockSpec((1, HEADS // 2, tk, LANES), lambda b, i: (b, 0, i, 0))],
        out_shape=[jax.ShapeDtypeStruct((B, HEADS, Sk, LANES), BF16),
                   jax.ShapeDtypeStruct((B, HEADS // 2, Sk, LANES), BF16)],
        compiler_params=pltpu.CompilerParams(
            dimension_semantics=("parallel", "parallel"), vmem_limit_bytes=VMEM_LIMIT),
        name="kv",
    )(*args)


def _attn_kernel(q_ref, k_ref, v_ref, o_ref, s_ref, p_ref, l_ref, *, chunks):
    j = pl.program_id(0)
    slot = j % 2
    prev = 1 - slot
    tq = q_ref.shape[2]
    lane = lax.broadcasted_iota(jnp.int32, (tq, LANES), 1)

    @pl.when(j == 0)
    def _():
        p_ref[1] = jnp.zeros(p_ref.shape[1:], BF16)
        l_ref[1] = jnp.ones(l_ref.shape[1:], F32)

    outs = []
    for sub in range(2):
        acc = jnp.dot(p_ref[prev, sub], v_ref[0, 0], preferred_element_type=F32)
        outs.append(acc / jnp.sum(l_ref[prev, sub], axis=-1, keepdims=True))
    o_ref[0] = jnp.where(lane < V_DIM, outs[0], outs[1]).astype(BF16)

    for sub in range(2):
        qh = q_ref[0, sub]
        m_part = jnp.full((tq, LANES), NEG_BIG, F32)
        for c0, cw in chunks:
            s = lax.dot_general(qh, k_ref[0, sub, c0:c0 + cw, :], (((1,), (1,)), ((), ())),
                                preferred_element_type=F32)
            s_ref[sub, :, c0:c0 + cw] = s
            for t in range(cw // LANES):
                m_part = jnp.maximum(m_part, s[:, t * LANES:(t + 1) * LANES])
        m = jnp.max(m_part, axis=-1, keepdims=True)
        l_part = jnp.zeros((tq, LANES), F32)
        for c0, cw in chunks:
            p = jnp.exp2(s_ref[sub, :, c0:c0 + cw] - m)
            for t in range(cw // LANES):
                l_part = l_part + p[:, t * LANES:(t + 1) * LANES]
            p_ref[slot, sub, :, c0:c0 + cw] = p.astype(BF16)
        l_ref[slot, sub] = l_part


def _attn(q, k, v, tq, tk):
    B, H, S, _ = q.shape
    Sk = k.shape[2]
    chunks = tuple((c0, min(tk, Sk - c0)) for c0 in range(0, Sk, tk))
    npair, nq = H // 2, S // tq
    n_items = B * npair * nq

    def item(j):
        b, r = j // (npair * nq), j % (npair * nq)
        return b, r // nq, r % nq

    def cur(j):
        return item(jnp.minimum(j, n_items - 1))

    def prv(j):
        return item(jnp.maximum(j - 1, 0))

    return pl.pallas_call(
        functools.partial(_attn_kernel, chunks=chunks),
        grid=(n_items + 1,),
        in_specs=[pl.BlockSpec((1, 2, tq, LANES), lambda j: (cur(j)[0], cur(j)[1], cur(j)[2], 0)),
                  pl.BlockSpec((1, 2, Sk, LANES), lambda j: (cur(j)[0], cur(j)[1], 0, 0)),
                  pl.BlockSpec((1, 1, Sk, LANES), lambda j: (prv(j)[0], prv(j)[1], 0, 0))],
        out_specs=pl.BlockSpec((1, tq, LANES), lambda j: (prv(j)[0], prv(j)[2], prv(j)[1])),
        out_shape=jax.ShapeDtypeStruct((B, S, H * V_DIM), BF16),
        scratch_shapes=[pltpu.VMEM((2, tq, Sk), F32), pltpu.VMEM((2, 2, tq, Sk), BF16),
                        pltpu.VMEM((2, 2, tq, LANES), F32)],
        compiler_params=pltpu.CompilerParams(
            dimension_semantics=("arbitrary",), vmem_limit_bytes=VMEM_LIMIT),
        name="attn",
    )(q, k, v)


def _merge_kernel(x_ref, mod_ref, g1_ref, g2_ref,
                  u_ref, up_ref, un_ref, p_ref, pp_ref, pn_ref, o_ref,
                  wg_ref, dw_ref, dwb_ref, lng_ref, lnb_ref, wco_ref,
                  wpool_ref, pscale_ref, wpo_ref, wmo_ref, wout_ref,
                  wr1_ref, wr2_ref, br_ref, h2e_in_ref,
                  x1_ref, h2e_ref, ubuf, pbuf, ybuf, mbuf, *, seq_len):
    del h2e_in_ref
    i = pl.program_id(1)
    n_i = pl.num_programs(1)
    tm = x_ref.shape[1]
    d = x_ref.shape[2]
    x = x_ref[0]
    shift1 = mod_ref[0, 0:1, :]
    scale1 = mod_ref[0, 1:2, :]
    gate1 = mod_ref[0, 2:3, :]
    shift2 = mod_ref[0, 3:4, :]
    scale2 = mod_ref[0, 4:5, :]
    h = (x * _rms(x, d) * g1_ref[...] * (1.0 + scale1) + shift1).astype(BF16)
    gates = jax.nn.sigmoid(jnp.dot(h, wg_ref[...], preferred_element_type=F32))

    has_prev = (i > 0).astype(F32)
    has_next = (i < n_i - 1).astype(F32)
    ubuf[0, 0:HALO, :] = up_ref[0] * has_prev
    ubuf[0, HALO:HALO + tm, :] = u_ref[0]
    ubuf[0, HALO + tm:, :] = un_ref[0] * has_next
    pbuf[0:HALO, :] = pp_ref[0] * has_prev
    pbuf[HALO:HALO + tm, :] = p_ref[0]
    pbuf[HALO + tm:, :] = pn_ref[0] * has_next
    n_sh = tm + 2 * HALO - SUBLANES
    for sft in range(1, SUBLANES):
        ubuf[sft, 0:n_sh, :] = ubuf[0, sft:sft + n_sh, :]

    rc = 32
    half = CONV_WIDTH // 2
    for r0 in range(0, tm, rc):
        acc = jnp.zeros((rc, CONV_DIM), F32) + dwb_ref[...]
        for k in range(CONV_WIDTH):
            off = HALO - half + k
            row = r0 + (off // SUBLANES) * SUBLANES
            acc = acc + dw_ref[k:k + 1, :] * ubuf[off % SUBLANES, row:row + rc, :]
        mu = jnp.mean(acc, axis=-1, keepdims=True)
        cen = acc - mu
        var = jnp.mean(cen * cen, axis=-1, keepdims=True)
        yn = cen * lax.rsqrt(var + EPS) * lng_ref[...] + lnb_ref[...]
        ybuf[r0:r0 + rc, :] = (yn * jax.nn.sigmoid(yn)).astype(BF16)
    y_conv = jnp.dot(ybuf[...], wco_ref[...], preferred_element_type=F32)

    t = i * tm + lax.broadcasted_iota(jnp.int32, (tm, 1), 0)
    for g, w in enumerate(POOL_WINDOWS):
        cols = slice(g * POOL_GROUP_DIM, (g + 1) * POOL_GROUP_DIM)
        s = pbuf[HALO - w // 2:HALO - w // 2 + tm, cols]
        for j in range(1, w):
            s = s + pbuf[HALO - w // 2 + j:HALO - w // 2 + j + tm, cols]
        cnt = (jnp.minimum(t + w // 2, seq_len) - jnp.maximum(t - w // 2, 0)).astype(F32)
        mbuf[:, cols] = (s / cnt - pbuf[HALO:HALO + tm, cols]).astype(BF16)
    y_pool = jnp.dot(mbuf[...], wpool_ref[...], preferred_element_type=F32) * pscale_ref[...]
    y_pool = jnp.dot(y_pool.astype(BF16), wpo_ref[...], preferred_element_type=F32)

    y_mla = jnp.dot(o_ref[0], wmo_ref[...], preferred_element_type=F32)

    mixed = (gates[:, 0:d] * y_conv + gates[:, d:2 * d] * y_pool + gates[:, 2 * d:3 * d] * y_mla)
    x1 = x + gate1 * jnp.dot(mixed.astype(BF16), wout_ref[...], preferred_element_type=F32)
    x1_ref[0] = x1

    h2 = x1 * _rms(x1, d) * g2_ref[...] * (1.0 + scale2) + shift2
    hi = h2.astype(BF16)
    h2e_ref[:, 0:d] = h2
    lo = (h2 - hi.astype(F32)).astype(BF16)
    r1 = jnp.dot(hi, wr1_ref[...], preferred_element_type=F32)
    r2 = jnp.dot(lo, wr2_ref[...], preferred_element_type=F32)
    logits = r1[:, 0:LANES] + r1[:, LANES:2 * LANES] + r2 + br_ref[...]

    lane = lax.broadcasted_iota(jnp.int32, (tm, LANES), 1)
    is_g = (lane >= N_EXPERTS) & (lane < N_EXPERTS + N_GROUPS)
    lg = jnp.where(is_g, logits, NEG_BIG)
    mg = jnp.max(lg, axis=-1, keepdims=True)
    eg = jnp.where(is_g, jnp.exp(lg - mg), 0.0)
    pg = eg / jnp.sum(eg, axis=-1, keepdims=True)
    pg_max = jnp.max(pg, axis=-1, keepdims=True)
    g_idx = jnp.min(jnp.where(is_g & (pg == pg_max), lane, 4 * LANES), axis=-1,
                    keepdims=True) - N_EXPERTS
    in_grp = (lane < N_EXPERTS) & ((lane >> 3) == g_idx)
    le = jnp.where(in_grp, logits, NEG_BIG)
    me = jnp.max(le, axis=-1, keepdims=True)
    ee = jnp.where(in_grp, jnp.exp(le - me), 0.0)
    pe = ee / jnp.sum(ee, axis=-1, keepdims=True)
    pe_m = jnp.where(in_grp, pe, -1.0)
    p1 = jnp.max(pe_m, axis=-1, keepdims=True)
    i1 = jnp.min(jnp.where(pe_m == p1, lane, 4 * LANES), axis=-1, keepdims=True)
    pe_m2 = jnp.where(lane == i1, -1.0, pe_m)
    p2 = jnp.max(pe_m2, axis=-1, keepdims=True)
    i2 = jnp.min(jnp.where(pe_m2 == p2, lane, 4 * LANES), axis=-1, keepdims=True)
    tot = p1 + p2
    w1 = pg_max * (p1 / tot)
    w2 = pg_max * (p2 / tot)
    first_low = i1 < i2
    e_lo = jnp.minimum(i1, i2) - g_idx * EXP_PER_GROUP
    e_hi = jnp.maximum(i1, i2) - g_idx * EXP_PER_GROUP
    cls = (g_idx * (EXP_PER_GROUP * EXP_PER_GROUP) + e_lo * EXP_PER_GROUP + e_hi).astype(F32)
    h2e_ref[:, d:d + LANES] = jnp.where(
        lane == 0, jnp.where(first_low, w1, w2),
        jnp.where(lane == 1, jnp.where(first_low, w2, w1), jnp.where(lane == 2, cls, 0.0)))


def _merge(x, mod6, g1, g2, u, pool, o, w, tm, h2e_prev, row0):
    B, S, D = x.shape
    Bm = mod6.shape[0]
    nh = tm // HALO
    nsb = S // tm
    blk0 = row0 // tm
    last_h = S // HALO - 1
    n_rows = h2e_prev.shape[0]
    mod_idx = (lambda b, i: (b, 0, 0)) if Bm > 1 else (lambda b, i: (0, 0, 0))

    def tok(width):
        return pl.BlockSpec((1, tm, width), lambda b, i: (b, i, 0))

    def prev(width):
        return pl.BlockSpec((1, HALO, width), lambda b, i: (b, jnp.maximum(i * nh - 1, 0), 0))

    def nxt(width):
        return pl.BlockSpec((1, HALO, width), lambda b, i: (b, jnp.minimum((i + 1) * nh, last_h), 0))

    weights = [w["wg"], w["dw"], w["dwb"], w["lng"], w["lnb"], w["wco"], w["wpool"], w["pscale"],
               w["wpo"], w["wmo"], w["wout"], w["wr1"], w["wr2"], w["br"]]
    in_specs = ([tok(D), pl.BlockSpec((1, 6, D), mod_idx), _const_spec((1, D)), _const_spec((1, D)),
                 tok(CONV_DIM), prev(CONV_DIM), nxt(CONV_DIM),
                 tok(POOL_DIM), prev(POOL_DIM), nxt(POOL_DIM), tok(HEADS * V_DIM)]
                + [_const_spec(a.shape) for a in weights] + [pl.BlockSpec(memory_space=pl.ANY)])
    args = [x, mod6, g1, g2, u, u, u, pool, pool, pool, o, *weights, h2e_prev]
    aliases = {len(args) - 1: 1}
    return pl.pallas_call(
        functools.partial(_merge_kernel, seq_len=S),
        grid=(B, nsb),
        in_specs=in_specs,
        out_specs=[tok(D), pl.BlockSpec((tm, D + LANES), lambda b, i: (blk0 + b * nsb + i, 0))],
        out_shape=[jax.ShapeDtypeStruct((B, S, D), F32),
                   jax.ShapeDtypeStruct((n_rows, D + LANES), F32)],
        scratch_shapes=[pltpu.VMEM((SUBLANES, tm + 2 * HALO, CONV_DIM), F32),
                        pltpu.VMEM((tm + 2 * HALO, POOL_DIM), F32),
                        pltpu.VMEM((tm, CONV_DIM), BF16),
                        pltpu.VMEM((tm, POOL_DIM), BF16)],
        input_output_aliases=aliases,
        compiler_params=pltpu.CompilerParams(
            dimension_semantics=("parallel", "arbitrary"), vmem_limit_bytes=VMEM_LIMIT),
        name="merge",
    )(*args)


def _route_kernel(cls_ref, ones_ref, utri_ref, ltri_ref, dest_ref, tile_ref, cnt, carry, off):
    ph = pl.program_id(0)
    j = pl.program_id(1)
    tb = cls_ref.shape[2]
    n_cls = cnt.shape[0]
    ntp = tile_ref.shape[1]
    cls_row = cls_ref[0]
    onehot = jnp.where(lax.broadcasted_iota(jnp.int32, (n_cls, tb), 0) == cls_row, 1.0, 0.0)
    oh = onehot.astype(BF16)
    blk = jnp.dot(oh, ones_ref[...], preferred_element_type=F32)

    @pl.when((ph == 0) & (j == 0))
    def _():
        cnt[...] = jnp.zeros_like(cnt)

    @pl.when(ph == 0)
    def _():
        cnt[...] += blk
        dest_ref[0, 0] = jnp.zeros((1, tb), jnp.int32)

    @pl.when((ph == 1) & (j == 0))
    def _():
        padded = jnp.floor((cnt[...] + (ROW_TILE - 1)) * (1.0 / ROW_TILE)) * ROW_TILE
        start = jnp.dot(ltri_ref[...], padded, preferred_element_type=F32,
                        precision=lax.Precision.HIGHEST)
        off[...] = start
        carry[...] = jnp.zeros_like(carry)
        end = jnp.concatenate([start + padded] * (ntp // LANES), axis=1)
        tile_start = (lax.broadcasted_iota(jnp.int32, (n_cls, ntp), 1) * ROW_TILE).astype(F32)
        tile_ref[...] = jnp.sum(jnp.where(end <= tile_start, 1.0, 0.0), axis=0,
                                keepdims=True).astype(jnp.int32)

    @pl.when(ph == 1)
    def _():
        excl = jnp.dot(oh, utri_ref[...], preferred_element_type=F32)
        base = jnp.concatenate([off[...] + carry[...]] * (tb // LANES), axis=1)
        dest_ref[0, 0] = jnp.sum(onehot * (excl + base), axis=0, keepdims=True).astype(jnp.int32)
        carry[...] += blk


def _route(cls, n_tiles_pad, tb):
    T = cls.shape[0]
    nblk = T // tb
    ones = jnp.ones((tb, LANES), BF16)
    utri = jnp.triu(jnp.ones((tb, tb), F32), k=1).astype(BF16)
    ltri = jnp.tril(jnp.ones((N_CLASSES, N_CLASSES), F32), k=-1)
    dest, tile_cls = pl.pallas_call(
        _route_kernel,
        grid=(2, nblk),
        in_specs=[pl.BlockSpec((1, 1, tb), lambda ph, j: (j, 0, 0)),
                  _const_spec(ones.shape), _const_spec(utri.shape), _const_spec(ltri.shape)],
        out_specs=[pl.BlockSpec((1, 1, 1, tb), lambda ph, j: (ph, j, 0, 0)),
                   pl.BlockSpec((1, n_tiles_pad), lambda ph, j: (0, 0))],
        out_shape=[jax.ShapeDtypeStruct((2, nblk, 1, tb), jnp.int32),
                   jax.ShapeDtypeStruct((1, n_tiles_pad), jnp.int32)],
        scratch_shapes=[pltpu.VMEM((N_CLASSES, LANES), F32)] * 3,
        compiler_params=pltpu.CompilerParams(
            dimension_semantics=("arbitrary", "arbitrary"), vmem_limit_bytes=VMEM_LIMIT),
        name="route",
    )(cls.reshape(nblk, 1, tb), ones, utri, ltri)
    return dest[1].reshape(T), tile_cls.reshape(n_tiles_pad)


def _row_copy_in(h_hbm, xbuf, sem, ids_ref, slot, n_tok):
    for r in range(ROW_TILE):
        src = jnp.minimum(ids_ref[0, 0, r], n_tok - 1)
        pltpu.make_async_copy(h_hbm.at[pl.ds(src, 1), :], xbuf.at[slot, pl.ds(r, 1), :],
                              sem.at[slot]).start()


def _experts_kernel(elo_ref, ehi_ref, idp_ref, idc_ref, idn_ref, h_hbm,
                    wg1_ref, wu1_ref, wd1_ref, wg2_ref, wu2_ref, wd2_ref,
                    y_hbm, xbuf, ybuf, gsem, ssem, *, n_tok):
    i = pl.program_id(0)
    slot = i % 2
    other = 1 - slot
    d = ybuf.shape[2]

    @pl.when(i == 0)
    def _():
        ybuf[...] = jnp.zeros_like(ybuf)
        _row_copy_in(h_hbm, xbuf, gsem, idc_ref, 0, n_tok)

    _row_copy_in(h_hbm, xbuf, gsem, idn_ref, other, n_tok)
    for r in range(ROW_TILE):
        pltpu.make_async_copy(ybuf.at[other, pl.ds(r, 1), :],
                              y_hbm.at[pl.ds(idp_ref[0, 0, r], 1), :], ssem.at[0]).start()
    pltpu.make_async_copy(h_hbm.at[pl.ds(0, ROW_TILE), :], xbuf.at[slot], gsem.at[slot]).wait()

    x = xbuf[slot]
    hb = x[:, 0:d].astype(BF16)
    w_lo = x[:, d:d + 1]
    w_hi = x[:, d + 1:d + 2]
    a = jnp.dot(hb, wg1_ref[0], preferred_element_type=F32)
    u = jnp.dot(hb, wu1_ref[0], preferred_element_type=F32)
    hid = ((a * jax.nn.sigmoid(a)) * u * w_lo).astype(BF16)
    y = jnp.dot(hid, wd1_ref[0], preferred_element_type=F32)
    a = jnp.dot(hb, wg2_ref[0], preferred_element_type=F32)
    u = jnp.dot(hb, wu2_ref[0], preferred_element_type=F32)
    hid = ((a * jax.nn.sigmoid(a)) * u * w_hi).astype(BF16)
    ybuf[slot] = y + jnp.dot(hid, wd2_ref[0], preferred_element_type=F32)

    pltpu.make_async_copy(ybuf.at[other], y_hbm.at[pl.ds(0, ROW_TILE), :], ssem.at[0]).wait()

    @pl.when(i == pl.num_programs(0) - 1)
    def _():
        pltpu.make_async_copy(h_hbm.at[pl.ds(0, ROW_TILE), :], xbuf.at[other],
                              gsem.at[other]).wait()


def _experts(h2e, ids_ext, e_lo, e_hi, wg, wu, wd, n_tok, n_steps):
    D = h2e.shape[1] - LANES
    E, _, F = wg.shape
    n_out = n_steps * ROW_TILE

    def ids_spec(shift):
        return pl.BlockSpec((1, 1, ROW_TILE), lambda i, lo, hi: (i + shift, 0, 0),
                            memory_space=pltpu.SMEM)

    def w_spec(shape, which):
        if which == 0:
            return pl.BlockSpec(shape, lambda i, lo, hi: (lo[i], 0, 0))
        return pl.BlockSpec(shape, lambda i, lo, hi: (hi[i], 0, 0))

    grid_spec = pltpu.PrefetchScalarGridSpec(
        num_scalar_prefetch=2,
        grid=(n_steps,),
        in_specs=[ids_spec(0), ids_spec(1), ids_spec(2), pl.BlockSpec(memory_space=pl.ANY),
                  w_spec((1, D, F), 0), w_spec((1, D, F), 0), w_spec((1, F, D), 0),
                  w_spec((1, D, F), 1), w_spec((1, D, F), 1), w_spec((1, F, D), 1)],
        out_specs=pl.BlockSpec(memory_space=pl.ANY),
        scratch_shapes=[pltpu.VMEM((2, ROW_TILE, D + LANES), F32),
                        pltpu.VMEM((2, ROW_TILE, D), F32),
                        pltpu.SemaphoreType.DMA((2,)),
                        pltpu.SemaphoreType.DMA((1,))])
    return pl.pallas_call(
        functools.partial(_experts_kernel, n_tok=n_tok),
        grid_spec=grid_spec,
        out_shape=jax.ShapeDtypeStruct((n_out, D), F32),
        compiler_params=pltpu.CompilerParams(
            dimension_semantics=("arbitrary",), vmem_limit_bytes=VMEM_LIMIT),
        name="experts",
    )(e_lo, e_hi, ids_ext, ids_ext, ids_ext, h2e, wg, wu, wd, wg, wu, wd)


def _resid_kernel(x1_ref, y_ref, mod_ref, o_ref):
    o_ref[0] = x1_ref[0] + mod_ref[0, 5:6, :] * y_ref[...]


def _resid(x1, y, mod6, row0, tm):
    B, S, D = x1.shape
    Bm = mod6.shape[0]
    nsb = S // tm
    blk0 = row0 // tm
    mod_idx = (lambda b, i: (b, 0, 0)) if Bm > 1 else (lambda b, i: (0, 0, 0))
    return pl.pallas_call(
        _resid_kernel,
        grid=(B, nsb),
        in_specs=[pl.BlockSpec((1, tm, D), lambda b, i: (b, i, 0)),
                  pl.BlockSpec((tm, D), lambda b, i: (blk0 + b * nsb + i, 0)),
                  pl.BlockSpec((1, 6, D), mod_idx)],
        out_specs=pl.BlockSpec((1, tm, D), lambda b, i: (b, i, 0)),
        out_shape=jax.ShapeDtypeStruct((B, S, D), F32),
        compiler_params=pltpu.CompilerParams(
            dimension_semantics=("parallel", "parallel"), vmem_limit_bytes=VMEM_LIMIT),
        name="resid",
    )(x1, y, mod6)


def _rope_tables(seq_len, n_ctx):
    rows = seq_len // GRID_W
    row = jnp.repeat(jnp.arange(rows), GRID_W).astype(F32)
    col = jnp.tile(jnp.arange(GRID_W), rows).astype(F32)
    half = ROPE // 2
    inv = 1.0 / (ROPE_BASE ** (jnp.arange(0, half, 2, dtype=F32) / half))
    ang = jnp.concatenate([row[:, None] * inv, col[:, None] * inv], axis=-1)
    cos, sin = jnp.cos(ang), jnp.sin(ang)
    q4 = ROPE // 4
    zero = jnp.zeros((seq_len, q4), F32)
    c_parts, s1_parts, s2_parts = [jnp.ones((seq_len, NOPE), F32)], [jnp.zeros((seq_len, NOPE), F32)], \
        [jnp.zeros((seq_len, NOPE), F32)]
    for hf in range(2):
        cs, sn = cos[:, hf * q4:(hf + 1) * q4], sin[:, hf * q4:(hf + 1) * q4]
        c_parts += [cs, cs]
        s1_parts += [zero, sn]
        s2_parts += [-sn, zero]
    pad = LANES - NOPE - ROPE
    c_parts.append(jnp.ones((seq_len, pad), F32))
    s1_parts.append(jnp.zeros((seq_len, pad), F32))
    s2_parts.append(jnp.zeros((seq_len, pad), F32))
    c, s1, s2 = (jnp.concatenate(p, axis=-1) for p in (c_parts, s1_parts, s2_parts))
    if n_ctx:
        c = jnp.concatenate([c, jnp.ones((n_ctx, LANES), F32)], axis=0)
        s1 = jnp.concatenate([s1, jnp.zeros((n_ctx, LANES), F32)], axis=0)
        s2 = jnp.concatenate([s2, jnp.zeros((n_ctx, LANES), F32)], axis=0)
    return c, s1, s2


def _rope_partner():
    lane = jnp.arange(LANES)
    q4 = ROPE // 4
    in_rope = (lane >= NOPE) & (lane < NOPE + ROPE)
    first = ((lane - NOPE) // q4) % 2 == 0
    partner = jnp.where(in_rope, jnp.where(first, lane + q4, lane - q4), lane)
    return partner, in_rope.astype(F32)


def _head_matrices():
    row_head = jnp.arange(HEADS * LANES) // LANES
    hsum = (row_head[:, None] == jnp.arange(LANES)[None, :]).astype(BF16)
    hbc = jnp.concatenate([hsum.T, hsum.T], axis=0)
    return hsum, hbc


def _layer_weights(l, p):
    d = p["w_in"].shape[1]
    w_in = p["w_in"][l]
    n_a = 2 * CONV_DIM + POOL_DIM + Q_RANK + KV_RANK
    wa = jnp.concatenate([w_in[:, :n_a], jnp.zeros((d, NOPE), F32), w_in[:, n_a:n_a + ROPE],
                          jnp.zeros((d, LANES - NOPE - ROPE), F32)], axis=1).astype(BF16)
    wg = w_in[:, n_a + ROPE:].astype(BF16)
    wuq3 = jnp.pad(p["w_uq"][l].reshape(Q_RANK, HEADS, QK_DIM), ((0, 0), (0, 0), (0, LANES - QK_DIM)))
    wuq = wuq3.reshape(Q_RANK, HEADS * LANES).astype(BF16)
    partner, is_rope = _rope_partner()
    wuq_sw = (wuq3[:, :, partner] * is_rope).reshape(Q_RANK, HEADS * LANES).astype(BF16)
    qgain = jnp.pad(p["q_head_g"][l], (0, LANES - QK_DIM))
    wukv = p["w_ukv"][l].reshape(KV_RANK, HEADS, NOPE + V_DIM)
    wuk = jnp.pad(wukv[:, :, :NOPE], ((0, 0), (0, 0), (0, LANES - NOPE))).reshape(
        KV_RANK, HEADS * LANES).astype(BF16)
    wuv = wukv[:, :, NOPE:].reshape(KV_RANK, HEADS * V_DIM).astype(BF16)
    pad_gain = lambda g: jnp.pad(g, (0, LANES - QK_DIM)).reshape(1, LANES)
    wpool = jnp.zeros((POOL_DIM, POOL_DIM), F32)
    for g in range(len(POOL_WINDOWS)):
        sl = slice(g * POOL_GROUP_DIM, (g + 1) * POOL_GROUP_DIM)
        wpool = wpool.at[sl, sl].set(p["w_pool"][l, g])
    wr = jnp.concatenate([p["w_router_e"][l], p["w_router_g"][l]], axis=1)
    wr = jnp.pad(wr, ((0, 0), (0, LANES - wr.shape[1])))
    wr_hi = wr.astype(BF16)
    wr_lo = (wr - wr_hi.astype(F32)).astype(BF16)
    br = jnp.concatenate([p["b_router_e"][l], p["b_router_g"][l]])
    br = jnp.pad(br, (0, LANES - br.shape[0])).reshape(1, LANES)
    return dict(
        g1=p["norm1_g"][l].reshape(1, d), g2=p["norm2_g"][l].reshape(1, d),
        wa=wa, wg=wg, wuq=wuq, wuq_rope=jnp.concatenate([wuq, wuq_sw], axis=1),
        qgain_partner=qgain[partner].reshape(1, LANES), wuk=wuk, wuv=wuv,
        qng=p["q_norm_g"][l].reshape(1, Q_RANK), kvg=p["kv_norm_g"][l].reshape(1, KV_RANK),
        qgain=pad_gain(p["q_head_g"][l]), kgain=pad_gain(p["k_head_g"][l]),
        dw=p["conv_dw"][l], dwb=p["conv_dw_b"][l].reshape(1, CONV_DIM),
        lng=p["conv_ln_g"][l].reshape(1, CONV_DIM), lnb=p["conv_ln_b"][l].reshape(1, CONV_DIM),
        wco=p["w_conv_out"][l].astype(BF16), wpool=wpool.astype(BF16),
        pscale=p["pool_scale"][l].reshape(1, POOL_DIM), wpo=p["w_pool_out"][l].astype(BF16),
        wmo=p["w_mla_out"][l].astype(BF16), wout=p["w_out"][l].astype(BF16),
        wr1=jnp.concatenate([wr_hi, wr_lo], axis=1), wr2=wr_hi, br=br,
        weg=p["w_exp_gate"][l].astype(BF16), weu=p["w_exp_up"][l].astype(BF16),
        wed=p["w_exp_down"][l].astype(BF16))


def _pick(n, pref):
    t = min(n, pref)
    while n % t:
        t //= 2
    return t


def _mixers(x, mod6, w, ctx_ckv, ctx_krp, rope_k, head_mats, h2e_prev, row0):
    B, S, D = x.shape
    tm = _pick(S, 512)
    if rope_k is not None:
        c, s1, s2 = (t[:S] for t in rope_k)
        rope_q = (c * w["qgain"], (s1 + s2) * w["qgain_partner"])
        wuq = w["wuq_rope"]
    else:
        rope_q, wuq = None, w["wuq"]
    u, pool, q, ckv, krp = _pre(x, mod6, w["g1"], w["wa"], w["qng"], w["kvg"], wuq, w["qgain"],
                                *head_mats, rope_q, tm)
    if ctx_ckv is not None:
        ckv_all = jnp.concatenate([ckv, ctx_ckv], axis=1)
        krp_all = jnp.concatenate([krp, ctx_krp], axis=1)
    else:
        ckv_all, krp_all = ckv, krp
    Sk = ckv_all.shape[1]
    k, v = _kv(ckv_all, krp_all, w["wuk"], w["wuv"], w["kgain"], rope_k, _pick(Sk, 256))
    o = _attn(q, k, v, tm, 512)
    x1, h2e = _merge(x, mod6, w["g1"], w["g2"], u, pool, o, w, _pick(S, 256), h2e_prev, row0)
    return x1, h2e, ckv, krp


def _moe(h2e, w):
    T = h2e.shape[0]
    D = h2e.shape[1] - LANES
    cls = h2e[:, D + 2].astype(jnp.int32)
    n_tiles = T // ROW_TILE + N_USED_CLASSES
    n_tiles_pad = -(-n_tiles // LANES) * LANES
    dest, tile_cls = _route(cls, n_tiles_pad, _pick(T, 512))
    tile_cls = tile_cls[:n_tiles]
    valid = tile_cls < N_CLASSES
    grp = tile_cls // (EXP_PER_GROUP * EXP_PER_GROUP)
    e_lo = grp * EXP_PER_GROUP + (tile_cls // EXP_PER_GROUP) % EXP_PER_GROUP
    e_hi = grp * EXP_PER_GROUP + tile_cls % EXP_PER_GROUP
    last = jnp.maximum(jnp.sum(valid.astype(jnp.int32)) - 1, 0)
    e_lo = jnp.where(valid, e_lo, e_lo[last])
    e_hi = jnp.where(valid, e_hi, e_hi[last])
    e_lo = jnp.concatenate([e_lo, e_lo[-1:]])
    e_hi = jnp.concatenate([e_hi, e_hi[-1:]])
    n_steps = n_tiles + 1
    ids = jnp.full((n_tiles * ROW_TILE,), T, jnp.int32).at[dest].set(jnp.arange(T, dtype=jnp.int32))
    ids = jnp.concatenate([jnp.full((ROW_TILE,), T, jnp.int32), ids])
    is_pad = ids >= T
    ids = jnp.where(is_pad, T - 1 + jnp.cumsum(is_pad.astype(jnp.int32)), ids)
    ids = jnp.concatenate([ids, jnp.full((2 * ROW_TILE,), T, jnp.int32)])
    return _experts(h2e, ids.reshape(n_steps + 2, 1, ROW_TILE), e_lo, e_hi,
                    w["weg"], w["weu"], w["wed"], T, n_steps)


def kernel(x_prompt, x_sample, cache_ckv, cache_kr, c, c_ctx, norm1_g, norm2_g, w_ada, b_ada, w_in, conv_dw, conv_dw_b, conv_ln_g, conv_ln_b, w_conv_out, w_pool, pool_scale, w_pool_out, q_norm_g, w_uq, kv_norm_g, w_ukv, q_head_g, k_head_g, w_mla_out, w_out, w_router_g, b_router_g, w_router_e, b_router_e, w_exp_gate, w_exp_up, w_exp_down):
    p = dict(norm1_g=norm1_g, norm2_g=norm2_g, w_in=w_in, conv_dw=conv_dw, conv_dw_b=conv_dw_b,
             conv_ln_g=conv_ln_g, conv_ln_b=conv_ln_b, w_conv_out=w_conv_out, w_pool=w_pool,
             pool_scale=pool_scale, w_pool_out=w_pool_out, q_norm_g=q_norm_g, w_uq=w_uq,
             kv_norm_g=kv_norm_g, w_ukv=w_ukv, q_head_g=q_head_g, k_head_g=k_head_g,
             w_mla_out=w_mla_out, w_out=w_out, w_router_g=w_router_g, b_router_g=b_router_g,
             w_router_e=w_router_e, b_router_e=b_router_e, w_exp_gate=w_exp_gate,
             w_exp_up=w_exp_up, w_exp_down=w_exp_down)
    depth, d = norm1_g.shape
    db, ds, _ = x_sample.shape
    past = cache_ckv.shape[2]

    rows = -(-(1 + db) // 8) * 8
    cvec = jnp.zeros((rows, d), F32).at[0].set(c_ctx).at[1:1 + db].set(c)
    mod = _ada(cvec, w_ada, b_ada)

    rope_k = _rope_tables(ds, past)
    head_mats = _head_matrices()
    cache_krp = jnp.pad(cache_kr, ((0, 0), (0, 0), (0, 0), (NOPE, LANES - NOPE - ROPE)))

    y_prompt, y_sample = x_prompt, x_sample
    t_ctx = x_prompt.shape[0] * x_prompt.shape[1]
    t_all = t_ctx + db * ds
    new_ckv, new_kr = [], []
    for l in range(depth):
        w = _layer_weights(l, p)
        mod_ctx = mod[l, 0:1].reshape(1, 6, d)
        mod_lat = mod[l, 1:1 + db].reshape(db, 6, d)
        x1_p, h2e, ckv_l, krp_l = _mixers(y_prompt, mod_ctx, w, None, None, None, head_mats,
                                          jnp.zeros((t_all, d + LANES), F32), 0)
        new_ckv.append(ckv_l)
        new_kr.append(krp_l[:, :, NOPE:NOPE + ROPE])
        x1_s, h2e, _, _ = _mixers(y_sample, mod_lat, w, cache_ckv[:, l], cache_krp[:, l],
                                  rope_k, head_mats, h2e, t_ctx)
        y = _moe(h2e, w)
        y_prompt = _resid(x1_p, y, mod_ctx, 0, _pick(x_prompt.shape[1], 512))
        y_sample = _resid(x1_s, y, mod_lat, t_ctx, _pick(ds, 512))
    return (y_prompt, y_sample, jnp.stack(new_ckv, axis=1), jnp.stack(new_kr, axis=1))
```

```python
import functools
import math

import jax
import jax.numpy as jnp
from jax import lax
from jax.experimental import pallas as pl
from jax.experimental.pallas import tpu as pltpu

F32 = jnp.float32
BF16 = jnp.bfloat16

LANES = 128
SUBLANES = 8
HALO = 16
EPS = 1e-6
GRID_W = 64
CONV_DIM = 512
CONV_WIDTH = 31
POOL_DIM = 512
POOL_WINDOWS = (2, 4, 8, 16)
POOL_GROUP_DIM = 128
HEADS = 8
NOPE = 64
ROPE = 32
QK_DIM = NOPE + ROPE
V_DIM = 64
Q_RANK = 256
KV_RANK = 128
ROPE_BASE = 10000.0
N_GROUPS = 4
EXP_PER_GROUP = 8
N_EXPERTS = N_GROUPS * EXP_PER_GROUP
N_CLASSES = N_GROUPS * EXP_PER_GROUP * EXP_PER_GROUP
N_USED_CLASSES = N_GROUPS * EXP_PER_GROUP * (EXP_PER_GROUP - 1) // 2
ROW_TILE = 128
VMEM_LIMIT = 56 * 1024 * 1024
NEG_BIG = -1e30
Q_SCALE = (QK_DIM ** -0.5) * math.log2(math.e)


def _const_spec(shape):
    nd = len(shape)
    return pl.BlockSpec(shape, lambda *_: (0,) * nd, pipeline_mode=pl.Buffered(1))


def _rms(x, n):
    return lax.rsqrt(jnp.sum(x * x, axis=-1, keepdims=True) * (1.0 / n) + EPS)


def _rope(x, c, s1, s2):
    return x * c + pltpu.roll(x, 8, 1) * s1 + pltpu.roll(x, LANES - 8, 1) * s2


def _ada_kernel(c_ref, w_ref, b_ref, o_ref):
    c = c_ref[...]
    s = c * jax.nn.sigmoid(c)
    o_ref[0] = jnp.dot(s, w_ref[0], preferred_element_type=F32,
                       precision=lax.Precision.HIGHEST) + b_ref[0]


def _ada(cvec, w_ada, b_ada):
    L, D, D6 = w_ada.shape
    rows = cvec.shape[0]
    tn = 1536
    return pl.pallas_call(
        _ada_kernel,
        grid=(L, D6 // tn),
        in_specs=[pl.BlockSpec((rows, D), lambda l, j: (0, 0)),
                  pl.BlockSpec((1, D, tn), lambda l, j: (l, 0, j)),
                  pl.BlockSpec((1, 1, tn), lambda l, j: (l, 0, j))],
        out_specs=pl.BlockSpec((1, rows, tn), lambda l, j: (l, 0, j)),
        out_shape=jax.ShapeDtypeStruct((L, rows, D6), F32),
        compiler_params=pltpu.CompilerParams(
            dimension_semantics=("parallel", "parallel"), vmem_limit_bytes=VMEM_LIMIT),
        name="ada",
    )(cvec, w_ada, b_ada.reshape(L, 1, D6))


def _pre_kernel(*refs, use_rope):
    if use_rope:
        (x_ref, mod_ref, g1_ref, wa_ref, qng_ref, kvg_ref, wuq_ref, qgain_ref, hsum_ref, hbc_ref,
         ta_ref, tb_ref, u_ref, pool_ref, q_ref, ckv_ref, krp_ref) = refs
    else:
        (x_ref, mod_ref, g1_ref, wa_ref, qng_ref, kvg_ref, wuq_ref, qgain_ref, hsum_ref, hbc_ref,
         u_ref, pool_ref, q_ref, ckv_ref, krp_ref) = refs
    x = x_ref[0]
    d = x.shape[-1]
    shift1 = mod_ref[0, 0:1, :]
    scale1 = mod_ref[0, 1:2, :]
    h = x * _rms(x, d) * g1_ref[...] * (1.0 + scale1) + shift1
    proj = jnp.dot(h.astype(BF16), wa_ref[...], preferred_element_type=F32)
    a = proj[:, 0:CONV_DIM]
    b = proj[:, CONV_DIM:2 * CONV_DIM]
    u_ref[0] = a * jax.nn.sigmoid(b)
    o = 2 * CONV_DIM
    pool_ref[0] = proj[:, o:o + POOL_DIM]
    o += POOL_DIM
    qd = proj[:, o:o + Q_RANK]
    o += Q_RANK
    kvd = proj[:, o:o + KV_RANK]
    o += KV_RANK
    krp_ref[0] = proj[:, o:o + LANES]
    ckv_ref[0] = kvd * _rms(kvd, KV_RANK) * kvg_ref[...]
    qn = (qd * _rms(qd, Q_RANK) * qng_ref[...]).astype(BF16)
    q2 = jnp.dot(qn, wuq_ref[...], preferred_element_type=F32)
    hw = HEADS * LANES
    q = q2[:, 0:hw]
    ssq = jnp.dot((q * q).astype(BF16), hsum_ref[...], preferred_element_type=F32)
    r = lax.rsqrt(ssq * (1.0 / QK_DIM) + EPS) * Q_SCALE
    r_hi = r.astype(BF16)
    r_lo = (r - r_hi.astype(F32)).astype(BF16)
    rb = jnp.dot(jnp.concatenate([r_hi, r_lo], axis=-1), hbc_ref[...], preferred_element_type=F32)
    for hd in range(HEADS):
        cols = slice(hd * LANES, (hd + 1) * LANES)
        if use_rope:
            qg = q[:, cols] * ta_ref[...] + q2[:, hw + hd * LANES:hw + (hd + 1) * LANES] * tb_ref[...]
        else:
            qg = q[:, cols] * qgain_ref[...]
        q_ref[0, hd] = (qg * rb[:, cols]).astype(BF16)


def _pre(x, mod6, g1, wa, qng, kvg, wuq, qgain, hsum, hbc, rope_tabs, tm):
    B, S, D = x.shape
    Bm = mod6.shape[0]
    use_rope = rope_tabs is not None
    mod_idx = (lambda b, i: (b, 0, 0)) if Bm > 1 else (lambda b, i: (0, 0, 0))
    in_specs = [pl.BlockSpec((1, tm, D), lambda b, i: (b, i, 0)),
                pl.BlockSpec((1, 6, D), mod_idx),
                _const_spec((1, D)),
                _const_spec(wa.shape),
                _const_spec((1, Q_RANK)),
                _const_spec((1, KV_RANK)),
                _const_spec(wuq.shape),
                _const_spec((1, LANES)),
                _const_spec(hsum.shape),
                _const_spec(hbc.shape)]
    args = [x, mod6, g1, wa, qng, kvg, wuq, qgain, hsum, hbc]
    if use_rope:
        in_specs += [pl.BlockSpec((tm, LANES), lambda b, i: (i, 0))] * 2
        args += list(rope_tabs)

    def tok(width):
        return pl.BlockSpec((1, tm, width), lambda b, i: (b, i, 0))

    return pl.pallas_call(
        functools.partial(_pre_kernel, use_rope=use_rope),
        grid=(B, S // tm),
        in_specs=in_specs,
        out_specs=[tok(CONV_DIM), tok(POOL_DIM),
                   pl.BlockSpec((1, HEADS, tm, LANES), lambda b, i: (b, 0, i, 0)),
                   tok(KV_RANK), tok(LANES)],
        out_shape=[jax.ShapeDtypeStruct((B, S, CONV_DIM), F32),
                   jax.ShapeDtypeStruct((B, S, POOL_DIM), F32),
                   jax.ShapeDtypeStruct((B, HEADS, S, LANES), BF16),
                   jax.ShapeDtypeStruct((B, S, KV_RANK), F32),
                   jax.ShapeDtypeStruct((B, S, LANES), F32)],
        compiler_params=pltpu.CompilerParams(
            dimension_semantics=("parallel", "parallel"), vmem_limit_bytes=VMEM_LIMIT),
        name="pre",
    )(*args)


def _kv_kernel(*refs, use_rope):
    if use_rope:
        (ckv_ref, krp_ref, wuk_ref, wuv_ref, kgain_ref, c_ref, s1_ref, s2_ref,
         k_ref, v_ref) = refs
    else:
        ckv_ref, krp_ref, wuk_ref, wuv_ref, kgain_ref, k_ref, v_ref = refs
    ckv = ckv_ref[0].astype(BF16)
    kn = jnp.dot(ckv, wuk_ref[...], preferred_element_type=F32)
    v = jnp.dot(ckv, wuv_ref[...], preferred_element_type=F32).astype(BF16)
    for pr in range(HEADS // 2):
        v_ref[0, pr] = v[:, pr * LANES:(pr + 1) * LANES]
    krp = krp_ref[0]
    kgain = kgain_ref[...]
    ssr = jnp.sum(krp * krp, axis=-1, keepdims=True)
    krg = krp * kgain
    if use_rope:
        krg = _rope(krg, c_ref[...], s1_ref[...], s2_ref[...])
    for hd in range(HEADS):
        knh = kn[:, hd * LANES:(hd + 1) * LANES]
        ssq = jnp.sum(knh * knh, axis=-1, keepdims=True) + ssr
        r = lax.rsqrt(ssq * (1.0 / QK_DIM) + EPS)
        k_ref[0, hd] = ((knh * kgain + krg) * r).astype(BF16)


def _kv(ckv, krp, wuk, wuv, kgain, rope_tabs, tk):
    B, Sk, _ = ckv.shape
    use_rope = rope_tabs is not None
    in_specs = [pl.BlockSpec((1, tk, KV_RANK), lambda b, i: (b, i, 0)),
                pl.BlockSpec((1, tk, LANES), lambda b, i: (b, i, 0)),
                _const_spec(wuk.shape), _const_spec(wuv.shape), _const_spec((1, LANES))]
    args = [ckv, krp, wuk, wuv, kgain]
    if use_rope:
        in_specs += [pl.BlockSpec((tk, LANES), lambda b, i: (i, 0))] * 3
        args += list(rope_tabs)
    return pl.pallas_call(
        functools.partial(_kv_kernel, use_rope=use_rope),
        grid=(B, Sk // tk),
        in_specs=in_specs,
        out_specs=[pl.BlockSpec((1, HEADS, tk, LANES), lambda b, i: (b, 0, i, 0)),
                   pl.BlockSpec((1, HEADS // 2, tk, LANES), lambda b, i: (b, 0, i, 0))],
        out_shape=[jax.ShapeDtypeStruct((B, HEADS, Sk, LANES), BF16),
                   jax.ShapeDtypeStruct((B, HEADS // 2, Sk, LANES), BF16)],
        compiler_params=pltpu.CompilerParams(
            dimension_semantics=("parallel", "parallel"), vmem_limit_bytes=VMEM_LIMIT),
        name="kv",
    )(*args)


def _attn_kernel(q_ref, k_ref, v_ref, o_ref, s_ref, p_ref, l_ref, *, chunks):
    j = pl.program_id(0)
    slot = j % 2
    prev = 1 - slot
    tq = q_ref.shape[2]
    lane = lax.broadcasted_iota(jnp.int32, (tq, LANES), 1)

    @pl.when(j == 0)
    def _():
        p_ref[1] = jnp.zeros(p_ref.shape[1:], BF16)
        l_ref[1] = jnp.ones(l_ref.shape[1:], F32)

    outs = []
    for sub in range(2):
        acc = jnp.dot(p_ref[prev, sub], v_ref[0, 0], preferred_element_type=F32)
        outs.append(acc / jnp.sum(l_ref[prev, sub], axis=-1, keepdims=True))
    o_ref[0] = jnp.where(lane < V_DIM, outs[0], outs[1]).astype(BF16)

    for sub in range(2):
        qh = q_ref[0, sub]
        m_part = jnp.full((tq, LANES), NEG_BIG, F32)
        for c0, cw in chunks:
            s = lax.dot_general(qh, k_ref[0, sub, c0:c0 + cw, :], (((1,), (1,)), ((), ())),
                                preferred_element_type=F32)
            s_ref[sub, :, c0:c0 + cw] = s
            for t in range(cw // LANES):
                m_part = jnp.maximum(m_part, s[:, t * LANES:(t + 1) * LANES])
        m = jnp.max(m_part, axis=-1, keepdims=True)
        l_part = jnp.zeros((tq, LANES), F32)
        for c0, cw in chunks:
            p = jnp.exp2(s_ref[sub, :, c0:c0 + cw] - m)
            for t in range(cw // LANES):
                l_part = l_part + p[:, t * LANES:(t + 1) * LANES]
            p_ref[slot, sub, :, c0:c0 + cw] = p.astype(BF16)
        l_ref[slot, sub] = l_part


def _attn(q, k, v, tq, tk):
    B, H, S, _ = q.shape
    Sk = k.shape[2]
    chunks = tuple((c0, min(tk, Sk - c0)) for c0 in range(0, Sk, tk))
    npair, nq = H // 2, S // tq
    n_items = B * npair * nq

    def item(j):
        b, r = j // (npair * nq), j % (npair * nq)
        return b, r // nq, r % nq

    def cur(j):
        return item(jnp.minimum(j, n_items - 1))

    def prv(j):
        return item(jnp.maximum(j - 1, 0))

    return pl.pallas_call(
        functools.partial(_attn_kernel, chunks=chunks),
        grid=(n_items + 1,),
        in_specs=[pl.BlockSpec((1, 2, tq, LANES), lambda j: (cur(j)[0], cur(j)[1], cur(j)[2], 0)),
                  pl.BlockSpec((1, 2, Sk, LANES), lambda j: (cur(j)[0], cur(j)[1], 0, 0)),
                  pl.BlockSpec((1, 1, Sk, LANES), lambda j: (prv(j)[0], prv(j)[1], 0, 0))],
        out_specs=pl.BlockSpec((1, tq, LANES), lambda j: (prv(j)[0], prv(j)[2], prv(j)[1])),
        out_shape=jax.ShapeDtypeStruct((B, S, H * V_DIM), BF16),
        scratch_shapes=[pltpu.VMEM((2, tq, Sk), F32), pltpu.VMEM((2, 2, tq, Sk), BF16),
                        pltpu.VMEM((2, 2, tq, LANES), F32)],
        compiler_params=pltpu.CompilerParams(
            dimension_semantics=("arbitrary",), vmem_limit_bytes=VMEM_LIMIT),
        name="attn",
    )(q, k, v)


def _merge_kernel(x_ref, mod_ref, g1_ref, g2_ref,
                  u_ref, up_ref, un_ref, p_ref, pp_ref, pn_ref, o_ref,
                  wg_ref, dw_ref, dwb_ref, lng_ref, lnb_ref, wco_ref,
                  wpool_ref, pscale_ref, wpo_ref, wmo_ref, wout_ref,
                  wr1_ref, wr2_ref, br_ref, h2e_in_ref,
                  x1_ref, h2e_ref, ubuf, pbuf, ybuf, mbuf, *, seq_len):
    del h2e_in_ref
    i = pl.program_id(1)
    n_i = pl.num_programs(1)
    tm = x_ref.shape[1]
    d = x_ref.shape[2]
    x = x_ref[0]
    shift1 = mod_ref[0, 0:1, :]
    scale1 = mod_ref[0, 1:2, :]
    gate1 = mod_ref[0, 2:3, :]
    shift2 = mod_ref[0, 3:4, :]
    scale2 = mod_ref[0, 4:5, :]
    h = (x * _rms(x, d) * g1_ref[...] * (1.0 + scale1) + shift1).astype(BF16)
    gates = jax.nn.sigmoid(jnp.dot(h, wg_ref[...], preferred_element_type=F32))

    has_prev = (i > 0).astype(F32)
    has_next = (i < n_i - 1).astype(F32)
    ubuf[0, 0:HALO, :] = up_ref[0] * has_prev
    ubuf[0, HALO:HALO + tm, :] = u_ref[0]
    ubuf[0, HALO + tm:, :] = un_ref[0] * has_next
    pbuf[0:HALO, :] = pp_ref[0] * has_prev
    pbuf[HALO:HALO + tm, :] = p_ref[0]
    pbuf[HALO + tm:, :] = pn_ref[0] * has_next
    n_sh = tm + 2 * HALO - SUBLANES
    for sft in range(1, SUBLANES):
        ubuf[sft, 0:n_sh, :] = ubuf[0, sft:sft + n_sh, :]

    rc = 32
    half = CONV_WIDTH // 2
    for r0 in range(0, tm, rc):
        acc = jnp.zeros((rc, CONV_DIM), F32) + dwb_ref[...]
        for k in range(CONV_WIDTH):
            off = HALO - half + k
            row = r0 + (off // SUBLANES) * SUBLANES
            acc = acc + dw_ref[k:k + 1, :] * ubuf[off % SUBLANES, row:row + rc, :]
        mu = jnp.mean(acc, axis=-1, keepdims=True)
        cen = acc - mu
        var = jnp.mean(cen * cen, axis=-1, keepdims=True)
        yn = cen * lax.rsqrt(var + EPS) * lng_ref[...] + lnb_ref[...]
        ybuf[r0:r0 + rc, :] = (yn * jax.nn.sigmoid(yn)).astype(BF16)
    y_conv = jnp.dot(ybuf[...], wco_ref[...], preferred_element_type=F32)

    t = i * tm + lax.broadcasted_iota(jnp.int32, (tm, 1), 0)
    for g, w in enumerate(POOL_WINDOWS):
        cols = slice(g * POOL_GROUP_DIM, (g + 1) * POOL_GROUP_DIM)
        s = pbuf[HALO - w // 2:HALO - w // 2 + tm, cols]
        for j in range(1, w):
            s = s + pbuf[HALO - w // 2 + j:HALO - w // 2 + j + tm, cols]
        cnt = (jnp.minimum(t + w // 2, seq_len) - jnp.maximum(t - w // 2, 0)).astype(F32)
        mbuf[:, cols] = (s / cnt - pbuf[HALO:HALO + tm, cols]).astype(BF16)
    y_pool = jnp.dot(mbuf[...], wpool_ref[...], preferred_element_type=F32) * pscale_ref[...]
    y_pool = jnp.dot(y_pool.astype(BF16), wpo_ref[...], preferred_element_type=F32)

    y_mla = jnp.dot(o_ref[0], wmo_ref[...], preferred_element_type=F32)

    mixed = (gates[:, 0:d] * y_conv + gates[:, d:2 * d] * y_pool + gates[:, 2 * d:3 * d] * y_mla)
    x1 = x + gate1 * jnp.dot(mixed.astype(BF16), wout_ref[...], preferred_element_type=F32)
    x1_ref[0] = x1

    h2 = x1 * _rms(x1, d) * g2_ref[...] * (1.0 + scale2) + shift2
    hi = h2.astype(BF16)
    h2e_ref[:, 0:d] = h2
    lo = (h2 - hi.astype(F32)).astype(BF16)
    r1 = jnp.dot(hi, wr1_ref[...], preferred_element_type=F32)
    r2 = jnp.dot(lo, wr2_ref[...], preferred_element_type=F32)
    logits = r1[:, 0:LANES] + r1[:, LANES:2 * LANES] + r2 + br_ref[...]

    lane = lax.broadcasted_iota(jnp.int32, (tm, LANES), 1)
    is_g = (lane >= N_EXPERTS) & (lane < N_EXPERTS + N_GROUPS)
    lg = jnp.where(is_g, logits, NEG_BIG)
    mg = jnp.max(lg, axis=-1, keepdims=True)
    eg = jnp.where(is_g, jnp.exp(lg - mg), 0.0)
    pg = eg / jnp.sum(eg, axis=-1, keepdims=True)
    pg_max = jnp.max(pg, axis=-1, keepdims=True)
    g_idx = jnp.min(jnp.where(is_g & (pg == pg_max), lane, 4 * LANES), axis=-1,
                    keepdims=True) - N_EXPERTS
    in_grp = (lane < N_EXPERTS) & ((lane >> 3) == g_idx)
    le = jnp.where(in_grp, logits, NEG_BIG)
    me = jnp.max(le, axis=-1, keepdims=True)
    ee = jnp.where(in_grp, jnp.exp(le - me), 0.0)
    pe = ee / jnp.sum(ee, axis=-1, keepdims=True)
    pe_m = jnp.where(in_grp, pe, -1.0)
    p1 = jnp.max(pe_m, axis=-1, keepdims=True)
    i1 = jnp.min(jnp.where(pe_m == p1, lane, 4 * LANES), axis=-1, keepdims=True)
    pe_m2 = jnp.where(lane == i1, -1.0, pe_m)
    p2 = jnp.max(pe_m2, axis=-1, keepdims=True)
    i2 = jnp.min(jnp.where(pe_m2 == p2, lane, 4 * LANES), axis=-1, keepdims=True)
    tot = p1 + p2
    w1 = pg_max * (p1 / tot)
    w2 = pg_max * (p2 / tot)
    first_low = i1 < i2
    e_lo = jnp.minimum(i1, i2) - g_idx * EXP_PER_GROUP
    e_hi = jnp.maximum(i1, i2) - g_idx * EXP_PER_GROUP
    cls = (g_idx * (EXP_PER_GROUP * EXP_PER_GROUP) + e_lo * EXP_PER_GROUP + e_hi).astype(F32)
    h2e_ref[:, d:d + LANES] = jnp.where(
        lane == 0, jnp.where(first_low, w1, w2),
        jnp.where(lane == 1, jnp.where(first_low, w2, w1), jnp.where(lane == 2, cls, 0.0)))


def _merge(x, mod6, g1, g2, u, pool, o, w, tm, h2e_prev, row0):
    B, S, D = x.shape
    Bm = mod6.shape[0]
    nh = tm // HALO
    nsb = S // tm
    blk0 = row0 // tm
    last_h = S // HALO - 1
    n_rows = h2e_prev.shape[0]
    mod_idx = (lambda b, i: (b, 0, 0)) if Bm > 1 else (lambda b, i: (0, 0, 0))

    def tok(width):
        return pl.BlockSpec((1, tm, width), lambda b, i: (b, i, 0))

    def prev(width):
        return pl.BlockSpec((1, HALO, width), lambda b, i: (b, jnp.maximum(i * nh - 1, 0), 0))

    def nxt(width):
        return pl.BlockSpec((1, HALO, width), lambda b, i: (b, jnp.minimum((i + 1) * nh, last_h), 0))

    weights = [w["wg"], w["dw"], w["dwb"], w["lng"], w["lnb"], w["wco"], w["wpool"], w["pscale"],
               w["wpo"], w["wmo"], w["wout"], w["wr1"], w["wr2"], w["br"]]
    in_specs = ([tok(D), pl.BlockSpec((1, 6, D), mod_idx), _const_spec((1, D)), _const_spec((1, D)),
                 tok(CONV_DIM), prev(CONV_DIM), nxt(CONV_DIM),
                 tok(POOL_DIM), prev(POOL_DIM), nxt(POOL_DIM), tok(HEADS * V_DIM)]
                + [_const_spec(a.shape) for a in weights] + [pl.BlockSpec(memory_space=pl.ANY)])
    args = [x, mod6, g1, g2, u, u, u, pool, pool, pool, o, *weights, h2e_prev]
    aliases = {len(args) - 1: 1}
    return pl.pallas_call(
        functools.partial(_merge_kernel, seq_len=S),
        grid=(B, nsb),
        in_specs=in_specs,
        out_specs=[tok(D), pl.BlockSpec((tm, D + LANES), lambda b, i: (blk0 + b * nsb + i, 0))],
        out_shape=[jax.ShapeDtypeStruct((B, S, D), F32),
                   jax.ShapeDtypeStruct((n_rows, D + LANES), F32)],
        scratch_shapes=[pltpu.VMEM((SUBLANES, tm + 2 * HALO, CONV_DIM), F32),
                        pltpu.VMEM((tm + 2 * HALO, POOL_DIM), F32),
                        pltpu.VMEM((tm, CONV_DIM), BF16),
                        pltpu.VMEM((tm, POOL_DIM), BF16)],
        input_output_aliases=aliases,
        compiler_params=pltpu.CompilerParams(
            dimension_semantics=("parallel", "arbitrary"), vmem_limit_bytes=VMEM_LIMIT),
        name="merge",
    )(*args)


def _route_kernel(cls_ref, ones_ref, utri_ref, ltri_ref, dest_ref, tile_ref, cnt, carry, off):
    ph = pl.program_id(0)
    j = pl.program_id(1)
    tb = cls_ref.shape[2]
    n_cls = cnt.shape[0]
    ntp = tile_ref.shape[1]
    cls_row = cls_ref[0]
    onehot = jnp.where(lax.broadcasted_iota(jnp.int32, (n_cls, tb), 0) == cls_row, 1.0, 0.0)
    oh = onehot.astype(BF16)
    blk = jnp.dot(oh, ones_ref[...], preferred_element_type=F32)

    @pl.when((ph == 0) & (j == 0))
    def _():
        cnt[...] = jnp.zeros_like(cnt)

    @pl.when(ph == 0)
    def _():
        cnt[...] += blk
        dest_ref[0, 0] = jnp.zeros((1, tb), jnp.int32)

    @pl.when((ph == 1) & (j == 0))
    def _():
        padded = jnp.floor((cnt[...] + (ROW_TILE - 1)) * (1.0 / ROW_TILE)) * ROW_TILE
        start = jnp.dot(ltri_ref[...], padded, preferred_element_type=F32,
                        precision=lax.Precision.HIGHEST)
        off[...] = start
        carry[...] = jnp.zeros_like(carry)
        end = jnp.concatenate([start + padded] * (ntp // LANES), axis=1)
        tile_start = (lax.broadcasted_iota(jnp.int32, (n_cls, ntp), 1) * ROW_TILE).astype(F32)
        tile_ref[...] = jnp.sum(jnp.where(end <= tile_start, 1.0, 0.0), axis=0,
                                keepdims=True).astype(jnp.int32)

    @pl.when(ph == 1)
    def _():
        excl = jnp.dot(oh, utri_ref[...], preferred_element_type=F32)
        base = jnp.concatenate([off[...] + carry[...]] * (tb // LANES), axis=1)
        dest_ref[0, 0] = jnp.sum(onehot * (excl + base), axis=0, keepdims=True).astype(jnp.int32)
        carry[...] += blk


def _route(cls, n_tiles_pad, tb):
    T = cls.shape[0]
    nblk = T // tb
    ones = jnp.ones((tb, LANES), BF16)
    utri = jnp.triu(jnp.ones((tb, tb), F32), k=1).astype(BF16)
    ltri = jnp.tril(jnp.ones((N_CLASSES, N_CLASSES), F32), k=-1)
    dest, tile_cls = pl.pallas_call(
        _route_kernel,
        grid=(2, nblk),
        in_specs=[pl.BlockSpec((1, 1, tb), lambda ph, j: (j, 0, 0)),
                  _const_spec(ones.shape), _const_spec(utri.shape), _const_spec(ltri.shape)],
        out_specs=[pl.BlockSpec((1, 1, 1, tb), lambda ph, j: (ph, j, 0, 0)),
                   pl.BlockSpec((1, n_tiles_pad), lambda ph, j: (0, 0))],
        out_shape=[jax.ShapeDtypeStruct((2, nblk, 1, tb), jnp.int32),
                   jax.ShapeDtypeStruct((1, n_tiles_pad), jnp.int32)],
        scratch_shapes=[pltpu.VMEM((N_CLASSES, LANES), F32)] * 3,
        compiler_params=pltpu.CompilerParams(
            dimension_semantics=("arbitrary", "arbitrary"), vmem_limit_bytes=VMEM_LIMIT),
        name="route",
    )(cls.reshape(nblk, 1, tb), ones, utri, ltri)
    return dest[1].reshape(T), tile_cls.reshape(n_tiles_pad)


def _row_copy_in(h_hbm, xbuf, sem, ids_ref, slot, n_tok):
    for r in range(ROW_TILE):
        src = jnp.minimum(ids_ref[0, 0, r], n_tok - 1)
        pltpu.make_async_copy(h_hbm.at[pl.ds(src, 1), :], xbuf.at[slot, pl.ds(r, 1), :],
                              sem.at[slot]).start(priority=1)


def _experts_kernel(elo_ref, ehi_ref, nused_ref, idp_ref, idc_ref, idn_ref, h_hbm,
                    wg1_ref, wu1_ref, wd1_ref, wg2_ref, wu2_ref, wd2_ref,
                    y_hbm, xbuf, ybuf, zbuf, gsem, ssem, *, n_tok):
    i = pl.program_id(0)
    n_used = nused_ref[0]
    slot = i % 2
    other = 1 - slot
    d = ybuf.shape[2]

    @pl.when(i == 0)
    def _():
        ybuf[...] = jnp.zeros_like(ybuf)
        zbuf[...] = jnp.zeros_like(zbuf)
        _row_copy_in(h_hbm, xbuf, gsem, idc_ref, 0, n_tok)

    @pl.when(i <= n_used)
    def _():
        for r in range(ROW_TILE):
            pltpu.make_async_copy(ybuf.at[other, pl.ds(r, 1), :],
                                  y_hbm.at[pl.ds(idp_ref[0, 0, r], 1), :],
                                  ssem.at[0]).start(priority=r % 2)
        _row_copy_in(h_hbm, xbuf, gsem, idn_ref, other, n_tok)
        pltpu.make_async_copy(h_hbm.at[pl.ds(0, ROW_TILE), :], xbuf.at[slot], gsem.at[slot]).wait()

        x = xbuf[slot]
        hb = x[:, 0:d].astype(BF16)
        w_lo = x[:, d:d + 1]
        w_hi = x[:, d + 1:d + 2]
        a = jnp.dot(hb, wg1_ref[0], preferred_element_type=F32)
        u = jnp.dot(hb, wu1_ref[0], preferred_element_type=F32)
        hid = ((a * jax.nn.sigmoid(a)) * u * w_lo).astype(BF16)
        y = jnp.dot(hid, wd1_ref[0], preferred_element_type=F32)
        a = jnp.dot(hb, wg2_ref[0], preferred_element_type=F32)
        u = jnp.dot(hb, wu2_ref[0], preferred_element_type=F32)
        hid = ((a * jax.nn.sigmoid(a)) * u * w_hi).astype(BF16)
        ybuf[slot] = y + jnp.dot(hid, wd2_ref[0], preferred_element_type=F32)

        pltpu.make_async_copy(ybuf.at[other], y_hbm.at[pl.ds(0, ROW_TILE), :], ssem.at[0]).wait()

    @pl.when(i == n_used)
    def _():
        pltpu.make_async_copy(h_hbm.at[pl.ds(0, ROW_TILE), :], xbuf.at[other],
                              gsem.at[other]).wait()

    @pl.when(i > n_used)
    def _():
        row0 = pl.multiple_of(i * ROW_TILE, ROW_TILE)
        fill = pltpu.make_async_copy(zbuf, y_hbm.at[pl.ds(row0, ROW_TILE), :], ssem.at[0])
        fill.start()
        fill.wait()


def _experts(h2e, ids_ext, e_lo, e_hi, n_used, wg, wu, wd, n_tok, n_steps):
    D = h2e.shape[1] - LANES
    E, _, F = wg.shape
    n_out = n_steps * ROW_TILE

    def ids_spec(shift):
        return pl.BlockSpec((1, 1, ROW_TILE), lambda i, lo, hi, nu: (i + shift, 0, 0),
                            memory_space=pltpu.SMEM)

    def w_spec(shape, which):
        if which == 0:
            return pl.BlockSpec(shape, lambda i, lo, hi, nu: (lo[i], 0, 0))
        return pl.BlockSpec(shape, lambda i, lo, hi, nu: (hi[i], 0, 0))

    grid_spec = pltpu.PrefetchScalarGridSpec(
        num_scalar_prefetch=3,
        grid=(n_steps,),
        in_specs=[ids_spec(0), ids_spec(1), ids_spec(2), pl.BlockSpec(memory_space=pl.ANY),
                  w_spec((1, D, F), 0), w_spec((1, D, F), 0), w_spec((1, F, D), 0),
                  w_spec((1, D, F), 1), w_spec((1, D, F), 1), w_spec((1, F, D), 1)],
        out_specs=pl.BlockSpec(memory_space=pl.ANY),
        scratch_shapes=[pltpu.VMEM((2, ROW_TILE, D + LANES), F32),
                        pltpu.VMEM((2, ROW_TILE, D), F32),
                        pltpu.VMEM((ROW_TILE, D), F32),
                        pltpu.SemaphoreType.DMA((2,)),
                        pltpu.SemaphoreType.DMA((1,))])
    return pl.pallas_call(
        functools.partial(_experts_kernel, n_tok=n_tok),
        grid_spec=grid_spec,
        out_shape=jax.ShapeDtypeStruct((n_out, D), F32),
        compiler_params=pltpu.CompilerParams(
            dimension_semantics=("arbitrary",), vmem_limit_bytes=VMEM_LIMIT),
        name="experts",
    )(e_lo, e_hi, n_used, ids_ext, ids_ext, ids_ext, h2e, wg, wu, wd, wg, wu, wd)


def _resid_kernel(x1_ref, y_ref, mod_ref, o_ref):
    o_ref[0] = x1_ref[0] + mod_ref[0, 5:6, :] * y_ref[...]


def _resid(x1, y, mod6, row0, tm):
    B, S, D = x1.shape
    Bm = mod6.shape[0]
    nsb = S // tm
    blk0 = row0 // tm
    mod_idx = (lambda b, i: (b, 0, 0)) if Bm > 1 else (lambda b, i: (0, 0, 0))
    return pl.pallas_call(
        _resid_kernel,
        grid=(B, nsb),
        in_specs=[pl.BlockSpec((1, tm, D), lambda b, i: (b, i, 0)),
                  pl.BlockSpec((tm, D), lambda b, i: (blk0 + b * nsb + i, 0)),
                  pl.BlockSpec((1, 6, D), mod_idx)],
        out_specs=pl.BlockSpec((1, tm, D), lambda b, i: (b, i, 0)),
        out_shape=jax.ShapeDtypeStruct((B, S, D), F32),
        compiler_params=pltpu.CompilerParams(
            dimension_semantics=("parallel", "parallel"), vmem_limit_bytes=VMEM_LIMIT),
        name="resid",
    )(x1, y, mod6)


def _rope_tables(seq_len, n_ctx):
    rows = seq_len // GRID_W
    row = jnp.repeat(jnp.arange(rows), GRID_W).astype(F32)
    col = jnp.tile(jnp.arange(GRID_W), rows).astype(F32)
    half = ROPE // 2
    inv = 1.0 / (ROPE_BASE ** (jnp.arange(0, half, 2, dtype=F32) / half))
    ang = jnp.concatenate([row[:, None] * inv, col[:, None] * inv], axis=-1)
    cos, sin = jnp.cos(ang), jnp.sin(ang)
    q4 = ROPE // 4
    zero = jnp.zeros((seq_len, q4), F32)
    c_parts, s1_parts, s2_parts = [jnp.ones((seq_len, NOPE), F32)], [jnp.zeros((seq_len, NOPE), F32)], \
        [jnp.zeros((seq_len, NOPE), F32)]
    for hf in range(2):
        cs, sn = cos[:, hf * q4:(hf + 1) * q4], sin[:, hf * q4:(hf + 1) * q4]
        c_parts += [cs, cs]
        s1_parts += [zero, sn]
        s2_parts += [-sn, zero]
    pad = LANES - NOPE - ROPE
    c_parts.append(jnp.ones((seq_len, pad), F32))
    s1_parts.append(jnp.zeros((seq_len, pad), F32))
    s2_parts.append(jnp.zeros((seq_len, pad), F32))
    c, s1, s2 = (jnp.concatenate(p, axis=-1) for p in (c_parts, s1_parts, s2_parts))
    if n_ctx:
        c = jnp.concatenate([c, jnp.ones((n_ctx, LANES), F32)], axis=0)
        s1 = jnp.concatenate([s1, jnp.zeros((n_ctx, LANES), F32)], axis=0)
        s2 = jnp.concatenate([s2, jnp.zeros((n_ctx, LANES), F32)], axis=0)
    return c, s1, s2


def _rope_partner():
    lane = jnp.arange(LANES)
    q4 = ROPE // 4
    in_rope = (lane >= NOPE) & (lane < NOPE + ROPE)
    first = ((lane - NOPE) // q4) % 2 == 0
    partner = jnp.where(in_rope, jnp.where(first, lane + q4, lane - q4), lane)
    return partner, in_rope.astype(F32)


def _head_matrices():
    row_head = jnp.arange(HEADS * LANES) // LANES
    hsum = (row_head[:, None] == jnp.arange(LANES)[None, :]).astype(BF16)
    hbc = jnp.concatenate([hsum.T, hsum.T], axis=0)
    return hsum, hbc


def _layer_weights(l, p):
    d = p["w_in"].shape[1]
    w_in = p["w_in"][l]
    n_a = 2 * CONV_DIM + POOL_DIM + Q_RANK + KV_RANK
    wa = jnp.concatenate([w_in[:, :n_a], jnp.zeros((d, NOPE), F32), w_in[:, n_a:n_a + ROPE],
                          jnp.zeros((d, LANES - NOPE - ROPE), F32)], axis=1).astype(BF16)
    wg = w_in[:, n_a + ROPE:].astype(BF16)
    wuq3 = jnp.pad(p["w_uq"][l].reshape(Q_RANK, HEADS, QK_DIM), ((0, 0), (0, 0), (0, LANES - QK_DIM)))
    wuq = wuq3.reshape(Q_RANK, HEADS * LANES).astype(BF16)
    partner, is_rope = _rope_partner()
    wuq_sw = (wuq3[:, :, partner] * is_rope).reshape(Q_RANK, HEADS * LANES).astype(BF16)
    qgain = jnp.pad(p["q_head_g"][l], (0, LANES - QK_DIM))
    wukv = p["w_ukv"][l].reshape(KV_RANK, HEADS, NOPE + V_DIM)
    wuk = jnp.pad(wukv[:, :, :NOPE], ((0, 0), (0, 0), (0, LANES - NOPE))).reshape(
        KV_RANK, HEADS * LANES).astype(BF16)
    wuv = wukv[:, :, NOPE:].reshape(KV_RANK, HEADS * V_DIM).astype(BF16)
    pad_gain = lambda g: jnp.pad(g, (0, LANES - QK_DIM)).reshape(1, LANES)
    wpool = jnp.zeros((POOL_DIM, POOL_DIM), F32)
    for g in range(len(POOL_WINDOWS)):
        sl = slice(g * POOL_GROUP_DIM, (g + 1) * POOL_GROUP_DIM)
        wpool = wpool.at[sl, sl].set(p["w_pool"][l, g])
    wr = jnp.concatenate([p["w_router_e"][l], p["w_router_g"][l]], axis=1)
    wr = jnp.pad(wr, ((0, 0), (0, LANES - wr.shape[1])))
    wr_hi = wr.astype(BF16)
    wr_lo = (wr - wr_hi.astype(F32)).astype(BF16)
    br = jnp.concatenate([p["b_router_e"][l], p["b_router_g"][l]])
    br = jnp.pad(br, (0, LANES - br.shape[0])).reshape(1, LANES)
    return dict(
        g1=p["norm1_g"][l].reshape(1, d), g2=p["norm2_g"][l].reshape(1, d),
        wa=wa, wg=wg, wuq=wuq, wuq_rope=jnp.concatenate([wuq, wuq_sw], axis=1),
        qgain_partner=qgain[partner].reshape(1, LANES), wuk=wuk, wuv=wuv,
        qng=p["q_norm_g"][l].reshape(1, Q_RANK), kvg=p["kv_norm_g"][l].reshape(1, KV_RANK),
        qgain=pad_gain(p["q_head_g"][l]), kgain=pad_gain(p["k_head_g"][l]),
        dw=p["conv_dw"][l], dwb=p["conv_dw_b"][l].reshape(1, CONV_DIM),
        lng=p["conv_ln_g"][l].reshape(1, CONV_DIM), lnb=p["conv_ln_b"][l].reshape(1, CONV_DIM),
        wco=p["w_conv_out"][l].astype(BF16), wpool=wpool.astype(BF16),
        pscale=p["pool_scale"][l].reshape(1, POOL_DIM), wpo=p["w_pool_out"][l].astype(BF16),
        wmo=p["w_mla_out"][l].astype(BF16), wout=p["w_out"][l].astype(BF16),
        wr1=jnp.concatenate([wr_hi, wr_lo], axis=1), wr2=wr_hi, br=br,
        weg=p["w_exp_gate"][l].astype(BF16), weu=p["w_exp_up"][l].astype(BF16),
        wed=p["w_exp_down"][l].astype(BF16))


def _pick(n, pref):
    t = min(n, pref)
    while n % t:
        t //= 2
    return t


def _mixers(x, mod6, w, ctx_ckv, ctx_krp, rope_k, head_mats, h2e_prev, row0):
    B, S, D = x.shape
    tm = _pick(S, 512)
    if rope_k is not None:
        c, s1, s2 = (t[:S] for t in rope_k)
        rope_q = (c * w["qgain"], (s1 + s2) * w["qgain_partner"])
        wuq = w["wuq_rope"]
    else:
        rope_q, wuq = None, w["wuq"]
    u, pool, q, ckv, krp = _pre(x, mod6, w["g1"], w["wa"], w["qng"], w["kvg"], wuq, w["qgain"],
                                *head_mats, rope_q, tm)
    if ctx_ckv is not None:
        ckv_all = jnp.concatenate([ckv, ctx_ckv], axis=1)
        krp_all = jnp.concatenate([krp, ctx_krp], axis=1)
    else:
        ckv_all, krp_all = ckv, krp
    Sk = ckv_all.shape[1]
    k, v = _kv(ckv_all, krp_all, w["wuk"], w["wuv"], w["kgain"], rope_k, _pick(Sk, 1088))
    o = _attn(q, k, v, tm, 512)
    x1, h2e = _merge(x, mod6, w["g1"], w["g2"], u, pool, o, w, _pick(S, 256), h2e_prev, row0)
    return x1, h2e, ckv, krp


def _moe(h2e, w):
    T = h2e.shape[0]
    D = h2e.shape[1] - LANES
    cls = h2e[:, D + 2].astype(jnp.int32)
    n_tiles = T // ROW_TILE + N_USED_CLASSES
    n_tiles_pad = -(-n_tiles // LANES) * LANES
    dest, tile_cls = _route(cls, n_tiles_pad, _pick(T, 512))
    tile_cls = tile_cls[:n_tiles]
    valid = tile_cls < N_CLASSES
    grp = tile_cls // (EXP_PER_GROUP * EXP_PER_GROUP)
    e_lo = grp * EXP_PER_GROUP + (tile_cls // EXP_PER_GROUP) % EXP_PER_GROUP
    e_hi = grp * EXP_PER_GROUP + tile_cls % EXP_PER_GROUP
    n_used = jnp.sum(valid.astype(jnp.int32))
    last = jnp.maximum(n_used - 1, 0)
    e_lo = jnp.where(valid, e_lo, e_lo[last])
    e_hi = jnp.where(valid, e_hi, e_hi[last])
    e_lo = jnp.concatenate([e_lo, e_lo[-1:]])
    e_hi = jnp.concatenate([e_hi, e_hi[-1:]])
    n_steps = n_tiles + 1
    ids = jnp.full((n_tiles * ROW_TILE,), T, jnp.int32).at[dest].set(jnp.arange(T, dtype=jnp.int32))
    ids = jnp.concatenate([jnp.full((ROW_TILE,), T, jnp.int32), ids])
    is_pad = ids >= T
    ids = jnp.where(is_pad, T - 1 + jnp.cumsum(is_pad.astype(jnp.int32)), ids)
    ids = jnp.concatenate([ids, jnp.full((2 * ROW_TILE,), T, jnp.int32)])
    return _experts(h2e, ids.reshape(n_steps + 2, 1, ROW_TILE), e_lo, e_hi, n_used.reshape(1),
                    w["weg"], w["weu"], w["wed"], T, n_steps)


def kernel(x_prompt, x_sample, cache_ckv, cache_kr, c, c_ctx, norm1_g, norm2_g, w_ada, b_ada, w_in, conv_dw, conv_dw_b, conv_ln_g, conv_ln_b, w_conv_out, w_pool, pool_scale, w_pool_out, q_norm_g, w_uq, kv_norm_g, w_ukv, q_head_g, k_head_g, w_mla_out, w_out, w_router_g, b_router_g, w_router_e, b_router_e, w_exp_gate, w_exp_up, w_exp_down):
    p = dict(norm1_g=norm1_g, norm2_g=norm2_g, w_in=w_in, conv_dw=conv_dw, conv_dw_b=conv_dw_b,
             conv_ln_g=conv_ln_g, conv_ln_b=conv_ln_b, w_conv_out=w_conv_out, w_pool=w_pool,
             pool_scale=pool_scale, w_pool_out=w_pool_out, q_norm_g=q_norm_g, w_uq=w_uq,
             kv_norm_g=kv_norm_g, w_ukv=w_ukv, q_head_g=q_head_g, k_head_g=k_head_g,
             w_mla_out=w_mla_out, w_out=w_out, w_router_g=w_router_g, b_router_g=b_router_g,
             w_router_e=w_router_e, b_router_e=b_router_e, w_exp_gate=w_exp_gate,
             w_exp_up=w_exp_up, w_exp_down=w_exp_down)
    depth, d = norm1_g.shape
    db, ds, _ = x_sample.shape
    past = cache_ckv.shape[2]

    rows = -(-(1 + db) // 8) * 8
    cvec = jnp.zeros((rows, d), F32).at[0].set(c_ctx).at[1:1 + db].set(c)
    mod = _ada(cvec, w_ada, b_ada)

    rope_k = _rope_tables(ds, past)
    head_mats = _head_matrices()
    cache_krp = jnp.pad(cache_kr, ((0, 0), (0, 0), (0, 0), (NOPE, LANES - NOPE - ROPE)))

    y_prompt, y_sample = x_prompt, x_sample
    t_ctx = x_prompt.shape[0] * x_prompt.shape[1]
    t_all = t_ctx + db * ds
    new_ckv, new_kr = [], []
    for l in range(depth):
        w = _layer_weights(l, p)
        mod_ctx = mod[l, 0:1].reshape(1, 6, d)
        mod_lat = mod[l, 1:1 + db].reshape(db, 6, d)
        x1_p, h2e, ckv_l, krp_l = _mixers(y_prompt, mod_ctx, w, None, None, None, head_mats,
                                          jnp.zeros((t_all, d + LANES), F32), 0)
        new_ckv.append(ckv_l)
        new_kr.append(krp_l[:, :, NOPE:NOPE + ROPE])
        x1_s, h2e, _, _ = _mixers(y_sample, mod_lat, w, cache_ckv[:, l], cache_krp[:, l],
                                  rope_k, head_mats, h2e, t_ctx)
        y = _moe(h2e, w)
        y_prompt = _resid(x1_p, y, mod_ctx, 0, _pick(x_prompt.shape[1], 512))
        y_sample = _resid(x1_s, y, mod_lat, t_ctx, _pick(ds, 512))
    return (y_prompt, y_sample, jnp.stack(new_ckv, axis=1), jnp.stack(new_kr, axis=1))
```

```python
import functools
import math

import jax
import jax.numpy as jnp
from jax import lax
from jax.experimental import pallas as pl
from jax.experimental.pallas import tpu as pltpu

F32 = jnp.float32
BF16 = jnp.bfloat16

LANES = 128
SUBLANES = 8
HALO = 16
EPS = 1e-6
GRID_W = 64
CONV_DIM = 512
CONV_WIDTH = 31
POOL_DIM = 512
POOL_WINDOWS = (2, 4, 8, 16)
POOL_GROUP_DIM = 128
HEADS = 8
NOPE = 64
ROPE = 32
QK_DIM = NOPE + ROPE
V_DIM = 64
Q_RANK = 256
KV_RANK = 128
ROPE_BASE = 10000.0
N_GROUPS = 4
EXP_PER_GROUP = 8
N_EXPERTS = N_GROUPS * EXP_PER_GROUP
N_CLASSES = N_GROUPS * EXP_PER_GROUP * EXP_PER_GROUP
N_USED_CLASSES = N_GROUPS * EXP_PER_GROUP * (EXP_PER_GROUP - 1) // 2
ROW_TILE = 128
RESID_ROWS = 256
VMEM_LIMIT = 56 * 1024 * 1024
NEG_BIG = -1e30
Q_SCALE = (QK_DIM ** -0.5) * math.log2(math.e)


def _const_spec(shape):
    nd = len(shape)
    return pl.BlockSpec(shape, lambda *_: (0,) * nd, pipeline_mode=pl.Buffered(1))


def _rms(x, n):
    return lax.rsqrt(jnp.sum(x * x, axis=-1, keepdims=True) * (1.0 / n) + EPS)


def _rope(x, c, s1, s2):
    return x * c + pltpu.roll(x, 8, 1) * s1 + pltpu.roll(x, LANES - 8, 1) * s2


def _ada_kernel(c_ref, w_ref, b_ref, o_ref):
    c = c_ref[...]
    s = c * jax.nn.sigmoid(c)
    o_ref[0] = jnp.dot(s, w_ref[0], preferred_element_type=F32,
                       precision=lax.Precision.HIGHEST) + b_ref[0]


def _ada(cvec, w_ada, b_ada):
    L, D, D6 = w_ada.shape
    rows = cvec.shape[0]
    tn = 1536
    return pl.pallas_call(
        _ada_kernel,
        grid=(L, D6 // tn),
        in_specs=[pl.BlockSpec((rows, D), lambda l, j: (0, 0)),
                  pl.BlockSpec((1, D, tn), lambda l, j: (l, 0, j)),
                  pl.BlockSpec((1, 1, tn), lambda l, j: (l, 0, j))],
        out_specs=pl.BlockSpec((1, rows, tn), lambda l, j: (l, 0, j)),
        out_shape=jax.ShapeDtypeStruct((L, rows, D6), F32),
        compiler_params=pltpu.CompilerParams(
            dimension_semantics=("parallel", "parallel"), vmem_limit_bytes=VMEM_LIMIT),
        name="ada",
    )(cvec, w_ada, b_ada.reshape(L, 1, D6))


def _pre_kernel(*refs, use_rope):
    if use_rope:
        (x_ref, mod_ref, g1_ref, wa_ref, qng_ref, kvg_ref, wuq_ref, qgain_ref, hsum_ref, hbc_ref,
         ta_ref, tb_ref, u_ref, pool_ref, q_ref, ckv_ref, krp_ref) = refs
    else:
        (x_ref, mod_ref, g1_ref, wa_ref, qng_ref, kvg_ref, wuq_ref, qgain_ref, hsum_ref, hbc_ref,
         u_ref, pool_ref, q_ref, ckv_ref, krp_ref) = refs
    x = x_ref[0]
    d = x.shape[-1]
    shift1 = mod_ref[0, 0:1, :]
    scale1 = mod_ref[0, 1:2, :]
    h = x * _rms(x, d) * g1_ref[...] * (1.0 + scale1) + shift1
    proj = jnp.dot(h.astype(BF16), wa_ref[...], preferred_element_type=F32)
    a = proj[:, 0:CONV_DIM]
    b = proj[:, CONV_DIM:2 * CONV_DIM]
    u_ref[0] = a * jax.nn.sigmoid(b)
    o = 2 * CONV_DIM
    pool_ref[0] = proj[:, o:o + POOL_DIM]
    o += POOL_DIM
    qd = proj[:, o:o + Q_RANK]
    o += Q_RANK
    kvd = proj[:, o:o + KV_RANK]
    o += KV_RANK
    krp_ref[0] = proj[:, o:o + LANES]
    ckv_ref[0] = kvd * _rms(kvd, KV_RANK) * kvg_ref[...]
    qn = (qd * _rms(qd, Q_RANK) * qng_ref[...]).astype(BF16)
    q2 = jnp.dot(qn, wuq_ref[...], preferred_element_type=F32)
    hw = HEADS * LANES
    q = q2[:, 0:hw]
    ssq = jnp.dot((q * q).astype(BF16), hsum_ref[...], preferred_element_type=F32)
    r = lax.rsqrt(ssq * (1.0 / QK_DIM) + EPS) * Q_SCALE
    r_hi = r.astype(BF16)
    r_lo = (r - r_hi.astype(F32)).astype(BF16)
    rb = jnp.dot(jnp.concatenate([r_hi, r_lo], axis=-1), hbc_ref[...], preferred_element_type=F32)
    for hd in range(HEADS):
        cols = slice(hd * LANES, (hd + 1) * LANES)
        if use_rope:
            qg = q[:, cols] * ta_ref[...] + q2[:, hw + hd * LANES:hw + (hd + 1) * LANES] * tb_ref[...]
        else:
            qg = q[:, cols] * qgain_ref[...]
        q_ref[0, hd] = (qg * rb[:, cols]).astype(BF16)


def _pre(x, mod6, g1, wa, qng, kvg, wuq, qgain, hsum, hbc, rope_tabs, tm):
    B, S, D = x.shape
    Bm = mod6.shape[0]
    use_rope = rope_tabs is not None
    mod_idx = (lambda b, i: (b, 0, 0)) if Bm > 1 else (lambda b, i: (0, 0, 0))
    in_specs = [pl.BlockSpec((1, tm, D), lambda b, i: (b, i, 0)),
                pl.BlockSpec((1, 6, D), mod_idx),
                _const_spec((1, D)),
                _const_spec(wa.shape),
                _const_spec((1, Q_RANK)),
                _const_spec((1, KV_RANK)),
                _const_spec(wuq.shape),
                _const_spec((1, LANES)),
                _const_spec(hsum.shape),
                _const_spec(hbc.shape)]
    args = [x, mod6, g1, wa, qng, kvg, wuq, qgain, hsum, hbc]
    if use_rope:
        in_specs += [pl.BlockSpec((tm, LANES), lambda b, i: (i, 0))] * 2
        args += list(rope_tabs)

    def tok(width):
        return pl.BlockSpec((1, tm, width), lambda b, i: (b, i, 0))

    return pl.pallas_call(
        functools.partial(_pre_kernel, use_rope=use_rope),
        grid=(B, S // tm),
        in_specs=in_specs,
        out_specs=[tok(CONV_DIM), tok(POOL_DIM),
                   pl.BlockSpec((1, HEADS, tm, LANES), lambda b, i: (b, 0, i, 0)),
                   tok(KV_RANK), tok(LANES)],
        out_shape=[jax.ShapeDtypeStruct((B, S, CONV_DIM), F32),
                   jax.ShapeDtypeStruct((B, S, POOL_DIM), F32),
                   jax.ShapeDtypeStruct((B, HEADS, S, LANES), BF16),
                   jax.ShapeDtypeStruct((B, S, KV_RANK), F32),
                   jax.ShapeDtypeStruct((B, S, LANES), F32)],
        compiler_params=pltpu.CompilerParams(
            dimension_semantics=("parallel", "parallel"), vmem_limit_bytes=VMEM_LIMIT),
        name="pre",
    )(*args)


def _kv_kernel(*refs, use_rope):
    if use_rope:
        (ckv_ref, krp_ref, wuk_ref, wuv_ref, kgain_ref, c_ref, s1_ref, s2_ref,
         k_ref, v_ref) = refs
    else:
        ckv_ref, krp_ref, wuk_ref, wuv_ref, kgain_ref, k_ref, v_ref = refs
    ckv = ckv_ref[0].astype(BF16)
    kn = jnp.dot(ckv, wuk_ref[...], preferred_element_type=F32)
    v = jnp.dot(ckv, wuv_ref[...], preferred_element_type=F32).astype(BF16)
    for pr in range(HEADS // 2):
        v_ref[0, pr] = v[:, pr * LANES:(pr + 1) * LANES]
    krp = krp_ref[0]
    kgain = kgain_ref[...]
    ssr = jnp.sum(krp * krp, axis=-1, keepdims=True)
    krg = krp * kgain
    if use_rope:
        krg = _rope(krg, c_ref[...], s1_ref[...], s2_ref[...])
    for hd in range(HEADS):
        knh = kn[:, hd * LANES:(hd + 1) * LANES]
        ssq = jnp.sum(knh * knh, axis=-1, keepdims=True) + ssr
        r = lax.rsqrt(ssq * (1.0 / QK_DIM) + EPS)
        k_ref[0, hd] = ((knh * kgain + krg) * r).astype(BF16)


def _kv(ckv, krp, wuk, wuv, kgain, rope_tabs, tk):
    B, Sk, _ = ckv.shape
    use_rope = rope_tabs is not None
    in_specs = [pl.BlockSpec((1, tk, KV_RANK), lambda b, i: (b, i, 0)),
                pl.BlockSpec((1, tk, LANES), lambda b, i: (b, i, 0)),
                _const_spec(wuk.shape), _const_spec(wuv.shape), _const_spec((1, LANES))]
    args = [ckv, krp, wuk, wuv, kgain]
    if use_rope:
        in_specs += [pl.BlockSpec((tk, LANES), lambda b, i: (i, 0))] * 3
        args += list(rope_tabs)
    return pl.pallas_call(
        functools.partial(_kv_kernel, use_rope=use_rope),
        grid=(B, Sk // tk),
        in_specs=in_specs,
        out_specs=[pl.BlockSpec((1, HEADS, tk, LANES), lambda b, i: (b, 0, i, 0)),
                   pl.BlockSpec((1, HEADS // 2, tk, LANES), lambda b, i: (b, 0, i, 0))],
        out_shape=[jax.ShapeDtypeStruct((B, HEADS, Sk, LANES), BF16),
                   jax.ShapeDtypeStruct((B, HEADS // 2, Sk, LANES), BF16)],
        compiler_params=pltpu.CompilerParams(
            dimension_semantics=("parallel", "parallel"), vmem_limit_bytes=VMEM_LIMIT),
        name="kv",
    )(*args)


def _attn_kernel(q_ref, k_ref, v_ref, o_ref, s_ref, p_ref, l_ref, *, chunks):
    j = pl.program_id(0)
    slot = j % 2
    prev = 1 - slot
    tq = q_ref.shape[2]
    lane = lax.broadcasted_iota(jnp.int32, (tq, LANES), 1)

    @pl.when(j == 0)
    def _():
        p_ref[1] = jnp.zeros(p_ref.shape[1:], BF16)
        l_ref[1] = jnp.ones(l_ref.shape[1:], F32)

    outs = []
    for sub in range(2):
        acc = jnp.dot(p_ref[prev, sub], v_ref[0, 0], preferred_element_type=F32)
        outs.append(acc / jnp.sum(l_ref[prev, sub], axis=-1, keepdims=True))
    o_ref[0] = jnp.where(lane < V_DIM, outs[0], outs[1]).astype(BF16)

    for sub in range(2):
        qh = q_ref[0, sub]
        m_part = jnp.full((tq, LANES), NEG_BIG, F32)
        for c0, cw in chunks:
            s = lax.dot_general(qh, k_ref[0, sub, c0:c0 + cw, :], (((1,), (1,)), ((), ())),
                                preferred_element_type=F32)
            s_ref[sub, :, c0:c0 + cw] = s
            for t in range(cw // LANES):
                m_part = jnp.maximum(m_part, s[:, t * LANES:(t + 1) * LANES])
        m = jnp.max(m_part, axis=-1, keepdims=True)
        l_part = jnp.zeros((tq, LANES), F32)
        for c0, cw in chunks:
            p = jnp.exp2(s_ref[sub, :, c0:c0 + cw] - m)
            for t in range(cw // LANES):
                l_part = l_part + p[:, t * LANES:(t + 1) * LANES]
            p_ref[slot, sub, :, c0:c0 + cw] = p.astype(BF16)
        l_ref[slot, sub] = l_part


def _attn(q, k, v, tq, tk):
    B, H, S, _ = q.shape
    Sk = k.shape[2]
    chunks = tuple((c0, min(tk, Sk - c0)) for c0 in range(0, Sk, tk))
    npair, nq = H // 2, S // tq
    n_items = B * npair * nq

    def item(j):
        b, r = j // (npair * nq), j % (npair * nq)
        return b, r // nq, r % nq

    def cur(j):
        return item(jnp.minimum(j, n_items - 1))

    def prv(j):
        return item(jnp.maximum(j - 1, 0))

    return pl.pallas_call(
        functools.partial(_attn_kernel, chunks=chunks),
        grid=(n_items + 1,),
        in_specs=[pl.BlockSpec((1, 2, tq, LANES), lambda j: (cur(j)[0], cur(j)[1], cur(j)[2], 0)),
                  pl.BlockSpec((1, 2, Sk, LANES), lambda j: (cur(j)[0], cur(j)[1], 0, 0)),
                  pl.BlockSpec((1, 1, Sk, LANES), lambda j: (prv(j)[0], prv(j)[1], 0, 0))],
        out_specs=pl.BlockSpec((1, tq, LANES), lambda j: (prv(j)[0], prv(j)[2], prv(j)[1])),
        out_shape=jax.ShapeDtypeStruct((B, S, H * V_DIM), BF16),
        scratch_shapes=[pltpu.VMEM((2, tq, Sk), F32), pltpu.VMEM((2, 2, tq, Sk), BF16),
                        pltpu.VMEM((2, 2, tq, LANES), F32)],
        compiler_params=pltpu.CompilerParams(
            dimension_semantics=("arbitrary",), vmem_limit_bytes=VMEM_LIMIT),
        name="attn",
    )(q, k, v)


def _merge_kernel(x_ref, mod_ref, g1_ref, g2_ref,
                  u_ref, up_ref, un_ref, p_ref, pp_ref, pn_ref, o_ref,
                  wg_ref, dw_ref, dwb_ref, lng_ref, lnb_ref, wco_ref,
                  wpool_ref, pscale_ref, wpo_ref, wmo_ref, wout_ref,
                  wr1_ref, wr2_ref, br_ref, h2e_in_ref,
                  x1_ref, h2e_ref, ubuf, pbuf, ybuf, mbuf, *, seq_len):
    del h2e_in_ref
    i = pl.program_id(1)
    n_i = pl.num_programs(1)
    tm = x_ref.shape[1]
    d = x_ref.shape[2]
    x = x_ref[0]
    shift1 = mod_ref[0, 0:1, :]
    scale1 = mod_ref[0, 1:2, :]
    gate1 = mod_ref[0, 2:3, :]
    shift2 = mod_ref[0, 3:4, :]
    scale2 = mod_ref[0, 4:5, :]
    h = (x * _rms(x, d) * g1_ref[...] * (1.0 + scale1) + shift1).astype(BF16)
    gates = jax.nn.sigmoid(jnp.dot(h, wg_ref[...], preferred_element_type=F32))

    has_prev = (i > 0).astype(F32)
    has_next = (i < n_i - 1).astype(F32)
    ubuf[0, 0:HALO, :] = up_ref[0] * has_prev
    ubuf[0, HALO:HALO + tm, :] = u_ref[0]
    ubuf[0, HALO + tm:, :] = un_ref[0] * has_next
    pbuf[0:HALO, :] = pp_ref[0] * has_prev
    pbuf[HALO:HALO + tm, :] = p_ref[0]
    pbuf[HALO + tm:, :] = pn_ref[0] * has_next
    n_sh = tm + 2 * HALO - SUBLANES
    for sft in range(1, SUBLANES):
        ubuf[sft, 0:n_sh, :] = ubuf[0, sft:sft + n_sh, :]

    rc = 32
    half = CONV_WIDTH // 2
    for r0 in range(0, tm, rc):
        acc = jnp.zeros((rc, CONV_DIM), F32) + dwb_ref[...]
        for k in range(CONV_WIDTH):
            off = HALO - half + k
            row = r0 + (off // SUBLANES) * SUBLANES
            acc = acc + dw_ref[k:k + 1, :] * ubuf[off % SUBLANES, row:row + rc, :]
        mu = jnp.mean(acc, axis=-1, keepdims=True)
        cen = acc - mu
        var = jnp.mean(cen * cen, axis=-1, keepdims=True)
        yn = cen * lax.rsqrt(var + EPS) * lng_ref[...] + lnb_ref[...]
        ybuf[r0:r0 + rc, :] = (yn * jax.nn.sigmoid(yn)).astype(BF16)
    y_conv = jnp.dot(ybuf[...], wco_ref[...], preferred_element_type=F32)

    t = i * tm + lax.broadcasted_iota(jnp.int32, (tm, 1), 0)
    for g, w in enumerate(POOL_WINDOWS):
        cols = slice(g * POOL_GROUP_DIM, (g + 1) * POOL_GROUP_DIM)
        s = pbuf[HALO - w // 2:HALO - w // 2 + tm, cols]
        for j in range(1, w):
            s = s + pbuf[HALO - w // 2 + j:HALO - w // 2 + j + tm, cols]
        cnt = (jnp.minimum(t + w // 2, seq_len) - jnp.maximum(t - w // 2, 0)).astype(F32)
        mbuf[:, cols] = (s / cnt - pbuf[HALO:HALO + tm, cols]).astype(BF16)
    y_pool = jnp.dot(mbuf[...], wpool_ref[...], preferred_element_type=F32) * pscale_ref[...]
    y_pool = jnp.dot(y_pool.astype(BF16), wpo_ref[...], preferred_element_type=F32)

    y_mla = jnp.dot(o_ref[0], wmo_ref[...], preferred_element_type=F32)

    mixed = (gates[:, 0:d] * y_conv + gates[:, d:2 * d] * y_pool + gates[:, 2 * d:3 * d] * y_mla)
    x1 = x + gate1 * jnp.dot(mixed.astype(BF16), wout_ref[...], preferred_element_type=F32)
    x1_ref[0] = x1

    h2 = x1 * _rms(x1, d) * g2_ref[...] * (1.0 + scale2) + shift2
    hi = h2.astype(BF16)
    h2e_ref[:, 0:d] = h2
    lo = (h2 - hi.astype(F32)).astype(BF16)
    r1 = jnp.dot(hi, wr1_ref[...], preferred_element_type=F32)
    r2 = jnp.dot(lo, wr2_ref[...], preferred_element_type=F32)
    logits = r1[:, 0:LANES] + r1[:, LANES:2 * LANES] + r2 + br_ref[...]

    lane = lax.broadcasted_iota(jnp.int32, (tm, LANES), 1)
    is_g = (lane >= N_EXPERTS) & (lane < N_EXPERTS + N_GROUPS)
    lg = jnp.where(is_g, logits, NEG_BIG)
    mg = jnp.max(lg, axis=-1, keepdims=True)
    eg = jnp.where(is_g, jnp.exp(lg - mg), 0.0)
    pg = eg / jnp.sum(eg, axis=-1, keepdims=True)
    pg_max = jnp.max(pg, axis=-1, keepdims=True)
    g_idx = jnp.min(jnp.where(is_g & (pg == pg_max), lane, 4 * LANES), axis=-1,
                    keepdims=True) - N_EXPERTS
    in_grp = (lane < N_EXPERTS) & ((lane >> 3) == g_idx)
    le = jnp.where(in_grp, logits, NEG_BIG)
    me = jnp.max(le, axis=-1, keepdims=True)
    ee = jnp.where(in_grp, jnp.exp(le - me), 0.0)
    pe = ee / jnp.sum(ee, axis=-1, keepdims=True)
    pe_m = jnp.where(in_grp, pe, -1.0)
    p1 = jnp.max(pe_m, axis=-1, keepdims=True)
    i1 = jnp.min(jnp.where(pe_m == p1, lane, 4 * LANES), axis=-1, keepdims=True)
    pe_m2 = jnp.where(lane == i1, -1.0, pe_m)
    p2 = jnp.max(pe_m2, axis=-1, keepdims=True)
    i2 = jnp.min(jnp.where(pe_m2 == p2, lane, 4 * LANES), axis=-1, keepdims=True)
    tot = p1 + p2
    w1 = pg_max * (p1 / tot)
    w2 = pg_max * (p2 / tot)
    first_low = i1 < i2
    e_lo = jnp.minimum(i1, i2) - g_idx * EXP_PER_GROUP
    e_hi = jnp.maximum(i1, i2) - g_idx * EXP_PER_GROUP
    cls = (g_idx * (EXP_PER_GROUP * EXP_PER_GROUP) + e_lo * EXP_PER_GROUP + e_hi).astype(F32)
    h2e_ref[:, d:d + LANES] = jnp.where(
        lane == 0, jnp.where(first_low, w1, w2),
        jnp.where(lane == 1, jnp.where(first_low, w2, w1), jnp.where(lane == 2, cls, 0.0)))


def _merge(x, mod6, g1, g2, u, pool, o, w, tm, h2e_prev, row0):
    B, S, D = x.shape
    Bm = mod6.shape[0]
    nh = tm // HALO
    nsb = S // tm
    blk0 = row0 // tm
    last_h = S // HALO - 1
    n_rows = h2e_prev.shape[0]
    mod_idx = (lambda b, i: (b, 0, 0)) if Bm > 1 else (lambda b, i: (0, 0, 0))

    def tok(width):
        return pl.BlockSpec((1, tm, width), lambda b, i: (b, i, 0))

    def prev(width):
        return pl.BlockSpec((1, HALO, width), lambda b, i: (b, jnp.maximum(i * nh - 1, 0), 0))

    def nxt(width):
        return pl.BlockSpec((1, HALO, width), lambda b, i: (b, jnp.minimum((i + 1) * nh, last_h), 0))

    weights = [w["wg"], w["dw"], w["dwb"], w["lng"], w["lnb"], w["wco"], w["wpool"], w["pscale"],
               w["wpo"], w["wmo"], w["wout"], w["wr1"], w["wr2"], w["br"]]
    in_specs = ([tok(D), pl.BlockSpec((1, 6, D), mod_idx), _const_spec((1, D)), _const_spec((1, D)),
                 tok(CONV_DIM), prev(CONV_DIM), nxt(CONV_DIM),
                 tok(POOL_DIM), prev(POOL_DIM), nxt(POOL_DIM), tok(HEADS * V_DIM)]
                + [_const_spec(a.shape) for a in weights] + [pl.BlockSpec(memory_space=pl.ANY)])
    args = [x, mod6, g1, g2, u, u, u, pool, pool, pool, o, *weights, h2e_prev]
    aliases = {len(args) - 1: 1}
    return pl.pallas_call(
        functools.partial(_merge_kernel, seq_len=S),
        grid=(B, nsb),
        in_specs=in_specs,
        out_specs=[tok(D), pl.BlockSpec((tm, D + LANES), lambda b, i: (blk0 + b * nsb + i, 0))],
        out_shape=[jax.ShapeDtypeStruct((B, S, D), F32),
                   jax.ShapeDtypeStruct((n_rows, D + LANES), F32)],
        scratch_shapes=[pltpu.VMEM((SUBLANES, tm + 2 * HALO, CONV_DIM), F32),
                        pltpu.VMEM((tm + 2 * HALO, POOL_DIM), F32),
                        pltpu.VMEM((tm, CONV_DIM), BF16),
                        pltpu.VMEM((tm, POOL_DIM), BF16)],
        input_output_aliases=aliases,
        compiler_params=pltpu.CompilerParams(
            dimension_semantics=("parallel", "arbitrary"), vmem_limit_bytes=VMEM_LIMIT),
        name="merge",
    )(*args)


def _route_kernel(cls_ref, ones_ref, utri_ref, ltri_ref, dest_ref, tile_ref, cnt, carry, off):
    ph = pl.program_id(0)
    j = pl.program_id(1)
    tb = cls_ref.shape[2]
    n_cls = cnt.shape[0]
    ntp = tile_ref.shape[1]
    cls_row = cls_ref[0]
    onehot = jnp.where(lax.broadcasted_iota(jnp.int32, (n_cls, tb), 0) == cls_row, 1.0, 0.0)
    oh = onehot.astype(BF16)
    blk = jnp.dot(oh, ones_ref[...], preferred_element_type=F32)

    @pl.when((ph == 0) & (j == 0))
    def _():
        cnt[...] = jnp.zeros_like(cnt)

    @pl.when(ph == 0)
    def _():
        cnt[...] += blk
        dest_ref[0, 0] = jnp.zeros((1, tb), jnp.int32)

    @pl.when((ph == 1) & (j == 0))
    def _():
        padded = jnp.floor((cnt[...] + (ROW_TILE - 1)) * (1.0 / ROW_TILE)) * ROW_TILE
        start = jnp.dot(ltri_ref[...], padded, preferred_element_type=F32,
                        precision=lax.Precision.HIGHEST)
        off[...] = start
        carry[...] = jnp.zeros_like(carry)
        end = jnp.concatenate([start + padded] * (ntp // LANES), axis=1)
        tile_start = (lax.broadcasted_iota(jnp.int32, (n_cls, ntp), 1) * ROW_TILE).astype(F32)
        tile_ref[...] = jnp.sum(jnp.where(end <= tile_start, 1.0, 0.0), axis=0,
                                keepdims=True).astype(jnp.int32)

    @pl.when(ph == 1)
    def _():
        excl = jnp.dot(oh, utri_ref[...], preferred_element_type=F32)
        base = jnp.concatenate([off[...] + carry[...]] * (tb // LANES), axis=1)
        dest_ref[0, 0] = jnp.sum(onehot * (excl + base), axis=0, keepdims=True).astype(jnp.int32)
        carry[...] += blk


def _route(cls, n_tiles_pad, tb):
    T = cls.shape[0]
    nblk = T // tb
    ones = jnp.ones((tb, LANES), BF16)
    utri = jnp.triu(jnp.ones((tb, tb), F32), k=1).astype(BF16)
    ltri = jnp.tril(jnp.ones((N_CLASSES, N_CLASSES), F32), k=-1)
    dest, tile_cls = pl.pallas_call(
        _route_kernel,
        grid=(2, nblk),
        in_specs=[pl.BlockSpec((1, 1, tb), lambda ph, j: (j, 0, 0)),
                  _const_spec(ones.shape), _const_spec(utri.shape), _const_spec(ltri.shape)],
        out_specs=[pl.BlockSpec((1, 1, 1, tb), lambda ph, j: (ph, j, 0, 0)),
                   pl.BlockSpec((1, n_tiles_pad), lambda ph, j: (0, 0))],
        out_shape=[jax.ShapeDtypeStruct((2, nblk, 1, tb), jnp.int32),
                   jax.ShapeDtypeStruct((1, n_tiles_pad), jnp.int32)],
        scratch_shapes=[pltpu.VMEM((N_CLASSES, LANES), F32)] * 3,
        compiler_params=pltpu.CompilerParams(
            dimension_semantics=("arbitrary", "arbitrary"), vmem_limit_bytes=VMEM_LIMIT),
        name="route",
    )(cls.reshape(nblk, 1, tb), ones, utri, ltri)
    return dest[1].reshape(T), tile_cls.reshape(n_tiles_pad)


def _row_copy_in(src_hbm, buf, sem, ids_ref, slot, n_rows, limit):
    for r in range(n_rows):
        src = jnp.minimum(ids_ref[0, 0, r], limit)
        pltpu.make_async_copy(src_hbm.at[pl.ds(src, 1), :], buf.at[slot, pl.ds(r, 1), :],
                              sem.at[slot]).start(priority=r % 2)


def _rows_wait(src_hbm, buf, sem, slot, n_rows):
    pltpu.make_async_copy(src_hbm.at[pl.ds(0, n_rows), :], buf.at[slot], sem.at[slot]).wait()


def _experts_kernel(elo_ref, ehi_ref, nused_ref, idc_ref, idn_ref, h_hbm,
                    wg1_ref, wu1_ref, wd1_ref, wg2_ref, wu2_ref, wd2_ref,
                    y_ref, xbuf, gsem, *, n_tok):
    i = pl.program_id(0)
    n_used = nused_ref[0]
    slot = i % 2
    other = 1 - slot
    d = y_ref.shape[1]

    @pl.when(i == 0)
    def _():
        _row_copy_in(h_hbm, xbuf, gsem, idc_ref, 0, ROW_TILE, n_tok - 1)

    @pl.when(i < n_used)
    def _():
        _row_copy_in(h_hbm, xbuf, gsem, idn_ref, other, ROW_TILE, n_tok - 1)
        _rows_wait(h_hbm, xbuf, gsem, slot, ROW_TILE)
        x = xbuf[slot]
        hb = x[:, 0:d].astype(BF16)
        w_lo = x[:, d:d + 1]
        w_hi = x[:, d + 1:d + 2]
        a = jnp.dot(hb, wg1_ref[0], preferred_element_type=F32)
        u = jnp.dot(hb, wu1_ref[0], preferred_element_type=F32)
        hid = ((a * jax.nn.sigmoid(a)) * u * w_lo).astype(BF16)
        y = jnp.dot(hid, wd1_ref[0], preferred_element_type=F32)
        a = jnp.dot(hb, wg2_ref[0], preferred_element_type=F32)
        u = jnp.dot(hb, wu2_ref[0], preferred_element_type=F32)
        hid = ((a * jax.nn.sigmoid(a)) * u * w_hi).astype(BF16)
        y_ref[...] = y + jnp.dot(hid, wd2_ref[0], preferred_element_type=F32)

    @pl.when(i == n_used - 1)
    def _():
        _rows_wait(h_hbm, xbuf, gsem, other, ROW_TILE)

    @pl.when(i >= n_used)
    def _():
        y_ref[...] = jnp.zeros_like(y_ref)


def _experts(h2e, ids, e_lo, e_hi, n_used, wg, wu, wd, n_tok):
    D = h2e.shape[1] - LANES
    E, _, F = wg.shape
    n_tiles = ids.shape[0] - 1

    def ids_spec(shift):
        return pl.BlockSpec((1, 1, ROW_TILE), lambda i, lo, hi, nu: (i + shift, 0, 0),
                            memory_space=pltpu.SMEM)

    def w_spec(shape, which):
        if which == 0:
            return pl.BlockSpec(shape, lambda i, lo, hi, nu: (lo[i], 0, 0))
        return pl.BlockSpec(shape, lambda i, lo, hi, nu: (hi[i], 0, 0))

    grid_spec = pltpu.PrefetchScalarGridSpec(
        num_scalar_prefetch=3,
        grid=(n_tiles,),
        in_specs=[ids_spec(0), ids_spec(1), pl.BlockSpec(memory_space=pl.ANY),
                  w_spec((1, D, F), 0), w_spec((1, D, F), 0), w_spec((1, F, D), 0),
                  w_spec((1, D, F), 1), w_spec((1, D, F), 1), w_spec((1, F, D), 1)],
        out_specs=pl.BlockSpec((ROW_TILE, D), lambda i, lo, hi, nu: (i, 0)),
        scratch_shapes=[pltpu.VMEM((2, ROW_TILE, D + LANES), F32),
                        pltpu.SemaphoreType.DMA((2,))])
    return pl.pallas_call(
        functools.partial(_experts_kernel, n_tok=n_tok),
        grid_spec=grid_spec,
        out_shape=jax.ShapeDtypeStruct((n_tiles * ROW_TILE, D), F32),
        compiler_params=pltpu.CompilerParams(
            dimension_semantics=("arbitrary",), vmem_limit_bytes=VMEM_LIMIT),
        name="experts",
    )(e_lo, e_hi, n_used, ids, ids, h2e, wg, wu, wd, wg, wu, wd)


def _resid_kernel(dc_ref, dn_ref, x1_ref, mod_ref, y_hbm, o_ref, buf, sem):
    i = pl.program_id(0)
    slot = i % 2
    other = 1 - slot
    tm = x1_ref.shape[0]
    limit = y_hbm.shape[0] - 1

    @pl.when(i == 0)
    def _():
        _row_copy_in(y_hbm, buf, sem, dc_ref, 0, tm, limit)

    @pl.when(i + 1 < pl.num_programs(0))
    def _():
        _row_copy_in(y_hbm, buf, sem, dn_ref, other, tm, limit)

    _rows_wait(y_hbm, buf, sem, slot, tm)
    o_ref[...] = x1_ref[...] + mod_ref[0, 5:6, :] * buf[slot]


def _resid(x1, y, dest, mod6, row0, tm):
    B, S, D = x1.shape
    Bm = mod6.shape[0]
    nsb = S // tm
    nblk = B * nsb
    blk0 = row0 // tm
    dest3 = dest.reshape(-1, 1, tm)
    last = blk0 + nblk - 1
    mod_idx = (lambda i: (lax.div(i, jnp.int32(nsb)), 0, 0)) if Bm > 1 else (lambda i: (0, 0, 0))
    out = pl.pallas_call(
        _resid_kernel,
        grid=(nblk,),
        in_specs=[pl.BlockSpec((1, 1, tm), lambda i: (jnp.minimum(blk0 + i, last), 0, 0),
                               memory_space=pltpu.SMEM),
                  pl.BlockSpec((1, 1, tm), lambda i: (jnp.minimum(blk0 + i + 1, last), 0, 0),
                               memory_space=pltpu.SMEM),
                  pl.BlockSpec((tm, D), lambda i: (i, 0)),
                  pl.BlockSpec((1, 6, D), mod_idx),
                  pl.BlockSpec(memory_space=pl.ANY)],
        out_specs=pl.BlockSpec((tm, D), lambda i: (i, 0)),
        out_shape=jax.ShapeDtypeStruct((B * S, D), F32),
        scratch_shapes=[pltpu.VMEM((2, tm, D), F32), pltpu.SemaphoreType.DMA((2,))],
        compiler_params=pltpu.CompilerParams(
            dimension_semantics=("arbitrary",), vmem_limit_bytes=VMEM_LIMIT),
        name="resid",
    )(dest3, dest3, x1.reshape(B * S, D), mod6, y)
    return out.reshape(B, S, D)


def _rope_tables(seq_len, n_ctx):
    rows = seq_len // GRID_W
    row = jnp.repeat(jnp.arange(rows), GRID_W).astype(F32)
    col = jnp.tile(jnp.arange(GRID_W), rows).astype(F32)
    half = ROPE // 2
    inv = 1.0 / (ROPE_BASE ** (jnp.arange(0, half, 2, dtype=F32) / half))
    ang = jnp.concatenate([row[:, None] * inv, col[:, None] * inv], axis=-1)
    cos, sin = jnp.cos(ang), jnp.sin(ang)
    q4 = ROPE // 4
    zero = jnp.zeros((seq_len, q4), F32)
    c_parts, s1_parts, s2_parts = [jnp.ones((seq_len, NOPE), F32)], [jnp.zeros((seq_len, NOPE), F32)], \
        [jnp.zeros((seq_len, NOPE), F32)]
    for hf in range(2):
        cs, sn = cos[:, hf * q4:(hf + 1) * q4], sin[:, hf * q4:(hf + 1) * q4]
        c_parts += [cs, cs]
        s1_parts += [zero, sn]
        s2_parts += [-sn, zero]
    pad = LANES - NOPE - ROPE
    c_parts.append(jnp.ones((seq_len, pad), F32))
    s1_parts.append(jnp.zeros((seq_len, pad), F32))
    s2_parts.append(jnp.zeros((seq_len, pad), F32))
    c, s1, s2 = (jnp.concatenate(p, axis=-1) for p in (c_parts, s1_parts, s2_parts))
    if n_ctx:
        c = jnp.concatenate([c, jnp.ones((n_ctx, LANES), F32)], axis=0)
        s1 = jnp.concatenate([s1, jnp.zeros((n_ctx, LANES), F32)], axis=0)
        s2 = jnp.concatenate([s2, jnp.zeros((n_ctx, LANES), F32)], axis=0)
    return c, s1, s2


def _rope_partner():
    lane = jnp.arange(LANES)
    q4 = ROPE // 4
    in_rope = (lane >= NOPE) & (lane < NOPE + ROPE)
    first = ((lane - NOPE) // q4) % 2 == 0
    partner = jnp.where(in_rope, jnp.where(first, lane + q4, lane - q4), lane)
    return partner, in_rope.astype(F32)


def _head_matrices():
    row_head = jnp.arange(HEADS * LANES) // LANES
    hsum = (row_head[:, None] == jnp.arange(LANES)[None, :]).astype(BF16)
    hbc = jnp.concatenate([hsum.T, hsum.T], axis=0)
    return hsum, hbc


def _layer_weights(l, p):
    d = p["w_in"].shape[1]
    w_in = p["w_in"][l]
    n_a = 2 * CONV_DIM + POOL_DIM + Q_RANK + KV_RANK
    wa = jnp.concatenate([w_in[:, :n_a], jnp.zeros((d, NOPE), F32), w_in[:, n_a:n_a + ROPE],
                          jnp.zeros((d, LANES - NOPE - ROPE), F32)], axis=1).astype(BF16)
    wg = w_in[:, n_a + ROPE:].astype(BF16)
    wuq3 = jnp.pad(p["w_uq"][l].reshape(Q_RANK, HEADS, QK_DIM), ((0, 0), (0, 0), (0, LANES - QK_DIM)))
    wuq = wuq3.reshape(Q_RANK, HEADS * LANES).astype(BF16)
    partner, is_rope = _rope_partner()
    wuq_sw = (wuq3[:, :, partner] * is_rope).reshape(Q_RANK, HEADS * LANES).astype(BF16)
    qgain = jnp.pad(p["q_head_g"][l], (0, LANES - QK_DIM))
    wukv = p["w_ukv"][l].reshape(KV_RANK, HEADS, NOPE + V_DIM)
    wuk = jnp.pad(wukv[:, :, :NOPE], ((0, 0), (0, 0), (0, LANES - NOPE))).reshape(
        KV_RANK, HEADS * LANES).astype(BF16)
    wuv = wukv[:, :, NOPE:].reshape(KV_RANK, HEADS * V_DIM).astype(BF16)
    pad_gain = lambda g: jnp.pad(g, (0, LANES - QK_DIM)).reshape(1, LANES)
    wpool = jnp.zeros((POOL_DIM, POOL_DIM), F32)
    for g in range(len(POOL_WINDOWS)):
        sl = slice(g * POOL_GROUP_DIM, (g + 1) * POOL_GROUP_DIM)
        wpool = wpool.at[sl, sl].set(p["w_pool"][l, g])
    wr = jnp.concatenate([p["w_router_e"][l], p["w_router_g"][l]], axis=1)
    wr = jnp.pad(wr, ((0, 0), (0, LANES - wr.shape[1])))
    wr_hi = wr.astype(BF16)
    wr_lo = (wr - wr_hi.astype(F32)).astype(BF16)
    br = jnp.concatenate([p["b_router_e"][l], p["b_router_g"][l]])
    br = jnp.pad(br, (0, LANES - br.shape[0])).reshape(1, LANES)
    return dict(
        g1=p["norm1_g"][l].reshape(1, d), g2=p["norm2_g"][l].reshape(1, d),
        wa=wa, wg=wg, wuq=wuq, wuq_rope=jnp.concatenate([wuq, wuq_sw], axis=1),
        qgain_partner=qgain[partner].reshape(1, LANES), wuk=wuk, wuv=wuv,
        qng=p["q_norm_g"][l].reshape(1, Q_RANK), kvg=p["kv_norm_g"][l].reshape(1, KV_RANK),
        qgain=pad_gain(p["q_head_g"][l]), kgain=pad_gain(p["k_head_g"][l]),
        dw=p["conv_dw"][l], dwb=p["conv_dw_b"][l].reshape(1, CONV_DIM),
        lng=p["conv_ln_g"][l].reshape(1, CONV_DIM), lnb=p["conv_ln_b"][l].reshape(1, CONV_DIM),
        wco=p["w_conv_out"][l].astype(BF16), wpool=wpool.astype(BF16),
        pscale=p["pool_scale"][l].reshape(1, POOL_DIM), wpo=p["w_pool_out"][l].astype(BF16),
        wmo=p["w_mla_out"][l].astype(BF16), wout=p["w_out"][l].astype(BF16),
        wr1=jnp.concatenate([wr_hi, wr_lo], axis=1), wr2=wr_hi, br=br,
        weg=p["w_exp_gate"][l].astype(BF16), weu=p["w_exp_up"][l].astype(BF16),
        wed=p["w_exp_down"][l].astype(BF16))


def _pick(n, pref):
    t = min(n, pref)
    while n % t:
        t //= 2
    return t


def _mixers(x, mod6, w, ctx_ckv, ctx_krp, rope_k, head_mats, h2e_prev, row0):
    B, S, D = x.shape
    tm = _pick(S, 512)
    if rope_k is not None:
        c, s1, s2 = (t[:S] for t in rope_k)
        rope_q = (c * w["qgain"], (s1 + s2) * w["qgain_partner"])
        wuq = w["wuq_rope"]
    else:
        rope_q, wuq = None, w["wuq"]
    u, pool, q, ckv, krp = _pre(x, mod6, w["g1"], w["wa"], w["qng"], w["kvg"], wuq, w["qgain"],
                                *head_mats, rope_q, tm)
    if ctx_ckv is not None:
        ckv_all = jnp.concatenate([ckv, ctx_ckv], axis=1)
        krp_all = jnp.concatenate([krp, ctx_krp], axis=1)
    else:
        ckv_all, krp_all = ckv, krp
    Sk = ckv_all.shape[1]
    k, v = _kv(ckv_all, krp_all, w["wuk"], w["wuv"], w["kgain"], rope_k, _pick(Sk, 1088))
    o = _attn(q, k, v, tm, 512)
    x1, h2e = _merge(x, mod6, w["g1"], w["g2"], u, pool, o, w, _pick(S, 256), h2e_prev, row0)
    return x1, h2e, ckv, krp


def _moe(h2e, w):
    T = h2e.shape[0]
    D = h2e.shape[1] - LANES
    cls = h2e[:, D + 2].astype(jnp.int32)
    n_tiles = T // ROW_TILE + N_USED_CLASSES
    n_tiles_pad = -(-n_tiles // LANES) * LANES
    dest, tile_cls = _route(cls, n_tiles_pad, _pick(T, 512))
    tile_cls = tile_cls[:n_tiles]
    valid = tile_cls < N_CLASSES
    grp = tile_cls // (EXP_PER_GROUP * EXP_PER_GROUP)
    e_lo = grp * EXP_PER_GROUP + (tile_cls // EXP_PER_GROUP) % EXP_PER_GROUP
    e_hi = grp * EXP_PER_GROUP + tile_cls % EXP_PER_GROUP
    n_used = jnp.sum(valid.astype(jnp.int32))
    last = jnp.maximum(n_used - 1, 0)
    e_lo = jnp.where(valid, e_lo, e_lo[last])
    e_hi = jnp.where(valid, e_hi, e_hi[last])
    ids = jnp.full(((n_tiles + 1) * ROW_TILE,), T, jnp.int32).at[dest].set(
        jnp.arange(T, dtype=jnp.int32))
    y = _experts(h2e, ids.reshape(n_tiles + 1, 1, ROW_TILE), e_lo, e_hi, n_used.reshape(1),
                 w["weg"], w["weu"], w["wed"], T)
    return y, dest


def kernel(x_prompt, x_sample, cache_ckv, cache_kr, c, c_ctx, norm1_g, norm2_g, w_ada, b_ada, w_in, conv_dw, conv_dw_b, conv_ln_g, conv_ln_b, w_conv_out, w_pool, pool_scale, w_pool_out, q_norm_g, w_uq, kv_norm_g, w_ukv, q_head_g, k_head_g, w_mla_out, w_out, w_router_g, b_router_g, w_router_e, b_router_e, w_exp_gate, w_exp_up, w_exp_down):
    p = dict(norm1_g=norm1_g, norm2_g=norm2_g, w_in=w_in, conv_dw=conv_dw, conv_dw_b=conv_dw_b,
             conv_ln_g=conv_ln_g, conv_ln_b=conv_ln_b, w_conv_out=w_conv_out, w_pool=w_pool,
             pool_scale=pool_scale, w_pool_out=w_pool_out, q_norm_g=q_norm_g, w_uq=w_uq,
             kv_norm_g=kv_norm_g, w_ukv=w_ukv, q_head_g=q_head_g, k_head_g=k_head_g,
             w_mla_out=w_mla_out, w_out=w_out, w_router_g=w_router_g, b_router_g=b_router_g,
             w_router_e=w_router_e, b_router_e=b_router_e, w_exp_gate=w_exp_gate,
             w_exp_up=w_exp_up, w_exp_down=w_exp_down)
    depth, d = norm1_g.shape
    db, ds, _ = x_sample.shape
    past = cache_ckv.shape[2]

    rows = -(-(1 + db) // 8) * 8
    cvec = jnp.zeros((rows, d), F32).at[0].set(c_ctx).at[1:1 + db].set(c)
    mod = _ada(cvec, w_ada, b_ada)

    rope_k = _rope_tables(ds, past)
    head_mats = _head_matrices()
    cache_krp = jnp.pad(cache_kr, ((0, 0), (0, 0), (0, 0), (NOPE, LANES - NOPE - ROPE)))

    y_prompt, y_sample = x_prompt, x_sample
    t_ctx = x_prompt.shape[0] * x_prompt.shape[1]
    t_all = t_ctx + db * ds
    new_ckv, new_kr = [], []
    for l in range(depth):
        w = _layer_weights(l, p)
        mod_ctx = mod[l, 0:1].reshape(1, 6, d)
        mod_lat = mod[l, 1:1 + db].reshape(db, 6, d)
        x1_p, h2e, ckv_l, krp_l = _mixers(y_prompt, mod_ctx, w, None, None, None, head_mats,
                                          jnp.zeros((t_all, d + LANES), F32), 0)
        new_ckv.append(ckv_l)
        new_kr.append(krp_l[:, :, NOPE:NOPE + ROPE])
        x1_s, h2e, _, _ = _mixers(y_sample, mod_lat, w, cache_ckv[:, l], cache_krp[:, l],
                                  rope_k, head_mats, h2e, t_ctx)
        y, dest = _moe(h2e, w)
        y_prompt = _resid(x1_p, y, dest, mod_ctx, 0, RESID_ROWS)
        y_sample = _resid(x1_s, y, dest, mod_lat, t_ctx, RESID_ROWS)
    return (y_prompt, y_sample, jnp.stack(new_ckv, axis=1), jnp.stack(new_kr, axis=1))
```

```python
import functools
import math

import jax
import jax.numpy as jnp
from jax import lax
from jax.experimental import pallas as pl
from jax.experimental.pallas import tpu as pltpu

F32 = jnp.float32
BF16 = jnp.bfloat16

LANES = 128
SUBLANES = 8
HALO = 16
EPS = 1e-6
GRID_W = 64
CONV_DIM = 512
CONV_WIDTH = 31
POOL_DIM = 512
POOL_WINDOWS = (2, 4, 8, 16)
POOL_GROUP_DIM = 128
HEADS = 8
NOPE = 64
ROPE = 32
QK_DIM = NOPE + ROPE
V_DIM = 64
Q_RANK = 256
KV_RANK = 128
ROPE_BASE = 10000.0
N_GROUPS = 4
EXP_PER_GROUP = 8
N_EXPERTS = N_GROUPS * EXP_PER_GROUP
N_CLASSES = N_GROUPS * EXP_PER_GROUP * EXP_PER_GROUP
N_USED_CLASSES = N_GROUPS * EXP_PER_GROUP * (EXP_PER_GROUP - 1) // 2
ROW_TILE = 128
RESID_ROWS = 256
VMEM_LIMIT = 56 * 1024 * 1024
NEG_BIG = -1e30
Q_SCALE = (QK_DIM ** -0.5) * math.log2(math.e)


def _const_spec(shape):
    nd = len(shape)
    return pl.BlockSpec(shape, lambda *_: (0,) * nd, pipeline_mode=pl.Buffered(1))


def _rms(x, n):
    return lax.rsqrt(jnp.sum(x * x, axis=-1, keepdims=True) * (1.0 / n) + EPS)


def _rope(x, c, s1, s2):
    return x * c + pltpu.roll(x, 8, 1) * s1 + pltpu.roll(x, LANES - 8, 1) * s2


def _ada_kernel(c_ref, w_ref, b_ref, o_ref):
    c = c_ref[...]
    s = c * jax.nn.sigmoid(c)
    o_ref[0] = jnp.dot(s, w_ref[0], preferred_element_type=F32,
                       precision=lax.Precision.HIGHEST) + b_ref[0]


def _ada(cvec, w_ada, b_ada):
    L, D, D6 = w_ada.shape
    rows = cvec.shape[0]
    tn = 1536
    return pl.pallas_call(
        _ada_kernel,
        grid=(L, D6 // tn),
        in_specs=[pl.BlockSpec((rows, D), lambda l, j: (0, 0)),
                  pl.BlockSpec((1, D, tn), lambda l, j: (l, 0, j)),
                  pl.BlockSpec((1, 1, tn), lambda l, j: (l, 0, j))],
        out_specs=pl.BlockSpec((1, rows, tn), lambda l, j: (l, 0, j)),
        out_shape=jax.ShapeDtypeStruct((L, rows, D6), F32),
        compiler_params=pltpu.CompilerParams(
            dimension_semantics=("parallel", "parallel"), vmem_limit_bytes=VMEM_LIMIT),
        name="ada",
    )(cvec, w_ada, b_ada.reshape(L, 1, D6))


def _pre_kernel(*refs, use_rope):
    if use_rope:
        (x_ref, mod_ref, g1_ref, wa_ref, qng_ref, kvg_ref, wuq_ref, qgain_ref, hsum_ref, hbc_ref,
         ta_ref, tb_ref, u_ref, pool_ref, q_ref, ckv_ref, krp_ref) = refs
    else:
        (x_ref, mod_ref, g1_ref, wa_ref, qng_ref, kvg_ref, wuq_ref, qgain_ref, hsum_ref, hbc_ref,
         u_ref, pool_ref, q_ref, ckv_ref, krp_ref) = refs
    x = x_ref[0]
    d = x.shape[-1]
    shift1 = mod_ref[0, 0:1, :]
    scale1 = mod_ref[0, 1:2, :]
    h = x * _rms(x, d) * g1_ref[...] * (1.0 + scale1) + shift1
    proj = jnp.dot(h.astype(BF16), wa_ref[...], preferred_element_type=F32)
    a = proj[:, 0:CONV_DIM]
    b = proj[:, CONV_DIM:2 * CONV_DIM]
    u_ref[0] = a * jax.nn.sigmoid(b)
    o = 2 * CONV_DIM
    pool_ref[0] = proj[:, o:o + POOL_DIM]
    o += POOL_DIM
    qd = proj[:, o:o + Q_RANK]
    o += Q_RANK
    kvd = proj[:, o:o + KV_RANK]
    o += KV_RANK
    krp_ref[0] = proj[:, o:o + LANES]
    ckv_ref[0] = kvd * _rms(kvd, KV_RANK) * kvg_ref[...]
    qn = (qd * _rms(qd, Q_RANK) * qng_ref[...]).astype(BF16)
    q2 = jnp.dot(qn, wuq_ref[...], preferred_element_type=F32)
    hw = HEADS * LANES
    q = q2[:, 0:hw]
    ssq = jnp.dot((q * q).astype(BF16), hsum_ref[...], preferred_element_type=F32)
    r = lax.rsqrt(ssq * (1.0 / QK_DIM) + EPS) * Q_SCALE
    r_hi = r.astype(BF16)
    r_lo = (r - r_hi.astype(F32)).astype(BF16)
    rb = jnp.dot(jnp.concatenate([r_hi, r_lo], axis=-1), hbc_ref[...], preferred_element_type=F32)
    for hd in range(HEADS):
        cols = slice(hd * LANES, (hd + 1) * LANES)
        if use_rope:
            qg = q[:, cols] * ta_ref[...] + q2[:, hw + hd * LANES:hw + (hd + 1) * LANES] * tb_ref[...]
        else:
            qg = q[:, cols] * qgain_ref[...]
        q_ref[0, hd] = (qg * rb[:, cols]).astype(BF16)


def _pre(x, mod6, g1, wa, qng, kvg, wuq, qgain, hsum, hbc, rope_tabs, tm):
    B, S, D = x.shape
    Bm = mod6.shape[0]
    use_rope = rope_tabs is not None
    mod_idx = (lambda b, i: (b, 0, 0)) if Bm > 1 else (lambda b, i: (0, 0, 0))
    in_specs = [pl.BlockSpec((1, tm, D), lambda b, i: (b, i, 0)),
                pl.BlockSpec((1, 6, D), mod_idx),
                _const_spec((1, D)),
                _const_spec(wa.shape),
                _const_spec((1, Q_RANK)),
                _const_spec((1, KV_RANK)),
                _const_spec(wuq.shape),
                _const_spec((1, LANES)),
                _const_spec(hsum.shape),
                _const_spec(hbc.shape)]
    args = [x, mod6, g1, wa, qng, kvg, wuq, qgain, hsum, hbc]
    if use_rope:
        in_specs += [pl.BlockSpec((tm, LANES), lambda b, i: (i, 0))] * 2
        args += list(rope_tabs)

    def tok(width):
        return pl.BlockSpec((1, tm, width), lambda b, i: (b, i, 0))

    return pl.pallas_call(
        functools.partial(_pre_kernel, use_rope=use_rope),
        grid=(B, S // tm),
        in_specs=in_specs,
        out_specs=[tok(CONV_DIM), tok(POOL_DIM),
                   pl.BlockSpec((1, HEADS, tm, LANES), lambda b, i: (b, 0, i, 0)),
                   tok(KV_RANK), tok(LANES)],
        out_shape=[jax.ShapeDtypeStruct((B, S, CONV_DIM), F32),
                   jax.ShapeDtypeStruct((B, S, POOL_DIM), F32),
                   jax.ShapeDtypeStruct((B, HEADS, S, LANES), BF16),
                   jax.ShapeDtypeStruct((B, S, KV_RANK), F32),
                   jax.ShapeDtypeStruct((B, S, LANES), F32)],
        compiler_params=pltpu.CompilerParams(
            dimension_semantics=("parallel", "parallel"), vmem_limit_bytes=VMEM_LIMIT),
        name="pre",
    )(*args)


def _kv_kernel(*refs, use_rope):
    if use_rope:
        (ckv_ref, krp_ref, wuk_ref, wuv_ref, kgain_ref, c_ref, s1_ref, s2_ref,
         k_ref, v_ref) = refs
    else:
        ckv_ref, krp_ref, wuk_ref, wuv_ref, kgain_ref, k_ref, v_ref = refs
    ckv = ckv_ref[0].astype(BF16)
    kn = jnp.dot(ckv, wuk_ref[...], preferred_element_type=F32)
    v = jnp.dot(ckv, wuv_ref[...], preferred_element_type=F32).astype(BF16)
    for pr in range(HEADS // 2):
        v_ref[0, pr] = v[:, pr * LANES:(pr + 1) * LANES]
    krp = krp_ref[0]
    kgain = kgain_ref[...]
    ssr = jnp.sum(krp * krp, axis=-1, keepdims=True)
    krg = krp * kgain
    if use_rope:
        krg = _rope(krg, c_ref[...], s1_ref[...], s2_ref[...])
    for hd in range(HEADS):
        knh = kn[:, hd * LANES:(hd + 1) * LANES]
        ssq = jnp.sum(knh * knh, axis=-1, keepdims=True) + ssr
        r = lax.rsqrt(ssq * (1.0 / QK_DIM) + EPS)
        k_ref[0, hd] = ((knh * kgain + krg) * r).astype(BF16)


def _kv(ckv, krp, wuk, wuv, kgain, rope_tabs, tk):
    B, Sk, _ = ckv.shape
    use_rope = rope_tabs is not None
    in_specs = [pl.BlockSpec((1, tk, KV_RANK), lambda b, i: (b, i, 0)),
                pl.BlockSpec((1, tk, LANES), lambda b, i: (b, i, 0)),
                _const_spec(wuk.shape), _const_spec(wuv.shape), _const_spec((1, LANES))]
    args = [ckv, krp, wuk, wuv, kgain]
    if use_rope:
        in_specs += [pl.BlockSpec((tk, LANES), lambda b, i: (i, 0))] * 3
        args += list(rope_tabs)
    return pl.pallas_call(
        functools.partial(_kv_kernel, use_rope=use_rope),
        grid=(B, Sk // tk),
        in_specs=in_specs,
        out_specs=[pl.BlockSpec((1, HEADS, tk, LANES), lambda b, i: (b, 0, i, 0)),
                   pl.BlockSpec((1, HEADS // 2, tk, LANES), lambda b, i: (b, 0, i, 0))],
        out_shape=[jax.ShapeDtypeStruct((B, HEADS, Sk, LANES), BF16),
                   jax.ShapeDtypeStruct((B, HEADS // 2, Sk, LANES), BF16)],
        compiler_params=pltpu.CompilerParams(
            dimension_semantics=("parallel", "parallel"), vmem_limit_bytes=VMEM_LIMIT),
        name="kv",
    )(*args)


def _attn_kernel(q_ref, k_ref, v_ref, o_ref, s_ref, p_ref, l_ref, *, chunks):
    j = pl.program_id(0)
    slot = j % 2
    prev = 1 - slot
    tq = q_ref.shape[2]
    lane = lax.broadcasted_iota(jnp.int32, (tq, LANES), 1)

    @pl.when(j == 0)
    def _():
        p_ref[1] = jnp.zeros(p_ref.shape[1:], BF16)
        l_ref[1] = jnp.ones(l_ref.shape[1:], F32)

    outs = []
    for sub in range(2):
        acc = jnp.dot(p_ref[prev, sub], v_ref[0, 0], preferred_element_type=F32)
        outs.append(acc / jnp.sum(l_ref[prev, sub], axis=-1, keepdims=True))
    o_ref[0] = jnp.where(lane < V_DIM, outs[0], outs[1]).astype(BF16)

    for sub in range(2):
        qh = q_ref[0, sub]
        m_part = jnp.full((tq, LANES), NEG_BIG, F32)
        for c0, cw in chunks:
            s = lax.dot_general(qh, k_ref[0, sub, c0:c0 + cw, :], (((1,), (1,)), ((), ())),
                                preferred_element_type=F32)
            s_ref[sub, :, c0:c0 + cw] = s
            for t in range(cw // LANES):
                m_part = jnp.maximum(m_part, s[:, t * LANES:(t + 1) * LANES])
        m = jnp.max(m_part, axis=-1, keepdims=True)
        l_part = jnp.zeros((tq, LANES), F32)
        for c0, cw in chunks:
            p = jnp.exp2(s_ref[sub, :, c0:c0 + cw] - m)
            for t in range(cw // LANES):
                l_part = l_part + p[:, t * LANES:(t + 1) * LANES]
            p_ref[slot, sub, :, c0:c0 + cw] = p.astype(BF16)
        l_ref[slot, sub] = l_part


def _attn(q, k, v, tq, tk):
    B, H, S, _ = q.shape
    Sk = k.shape[2]
    chunks = tuple((c0, min(tk, Sk - c0)) for c0 in range(0, Sk, tk))
    npair, nq = H // 2, S // tq
    n_items = B * npair * nq

    def item(j):
        b, r = j // (npair * nq), j % (npair * nq)
        return b, r // nq, r % nq

    def cur(j):
        return item(jnp.minimum(j, n_items - 1))

    def prv(j):
        return item(jnp.maximum(j - 1, 0))

    return pl.pallas_call(
        functools.partial(_attn_kernel, chunks=chunks),
        grid=(n_items + 1,),
        in_specs=[pl.BlockSpec((1, 2, tq, LANES), lambda j: (cur(j)[0], cur(j)[1], cur(j)[2], 0)),
                  pl.BlockSpec((1, 2, Sk, LANES), lambda j: (cur(j)[0], cur(j)[1], 0, 0)),
                  pl.BlockSpec((1, 1, Sk, LANES), lambda j: (prv(j)[0], prv(j)[1], 0, 0))],
        out_specs=pl.BlockSpec((1, tq, LANES), lambda j: (prv(j)[0], prv(j)[2], prv(j)[1])),
        out_shape=jax.ShapeDtypeStruct((B, S, H * V_DIM), BF16),
        scratch_shapes=[pltpu.VMEM((2, tq, Sk), F32), pltpu.VMEM((2, 2, tq, Sk), BF16),
                        pltpu.VMEM((2, 2, tq, LANES), F32)],
        compiler_params=pltpu.CompilerParams(
            dimension_semantics=("arbitrary",), vmem_limit_bytes=VMEM_LIMIT),
        name="attn",
    )(q, k, v)


def _merge_kernel(x_ref, mod_ref, g1_ref, g2_ref,
                  u_ref, up_ref, un_ref, p_ref, pp_ref, pn_ref, o_ref,
                  wg_ref, dw_ref, dwb_ref, lng_ref, lnb_ref, wco_ref,
                  wpool_ref, pscale_ref, wpo_ref, wmo_ref, wout_ref,
                  wr1_ref, wr2_ref, br_ref, h2e_in_ref,
                  x1_ref, h2e_ref, ubuf, pbuf, ybuf, mbuf, *, seq_len):
    del h2e_in_ref
    i = pl.program_id(1)
    n_i = pl.num_programs(1)
    tm = x_ref.shape[1]
    d = x_ref.shape[2]
    x = x_ref[0]
    shift1 = mod_ref[0, 0:1, :]
    scale1 = mod_ref[0, 1:2, :]
    gate1 = mod_ref[0, 2:3, :]
    shift2 = mod_ref[0, 3:4, :]
    scale2 = mod_ref[0, 4:5, :]
    h = (x * _rms(x, d) * g1_ref[...] * (1.0 + scale1) + shift1).astype(BF16)
    gates = jax.nn.sigmoid(jnp.dot(h, wg_ref[...], preferred_element_type=F32))

    has_prev = (i > 0).astype(F32)
    has_next = (i < n_i - 1).astype(F32)
    ubuf[0, 0:HALO, :] = up_ref[0] * has_prev
    ubuf[0, HALO:HALO + tm, :] = u_ref[0]
    ubuf[0, HALO + tm:, :] = un_ref[0] * has_next
    pbuf[0:HALO, :] = pp_ref[0] * has_prev
    pbuf[HALO:HALO + tm, :] = p_ref[0]
    pbuf[HALO + tm:, :] = pn_ref[0] * has_next
    n_sh = tm + 2 * HALO - SUBLANES
    for sft in range(1, SUBLANES):
        ubuf[sft, 0:n_sh, :] = ubuf[0, sft:sft + n_sh, :]

    rc = 32
    half = CONV_WIDTH // 2
    for r0 in range(0, tm, rc):
        acc = jnp.zeros((rc, CONV_DIM), F32) + dwb_ref[...]
        for k in range(CONV_WIDTH):
            off = HALO - half + k
            row = r0 + (off // SUBLANES) * SUBLANES
            acc = acc + dw_ref[k:k + 1, :] * ubuf[off % SUBLANES, row:row + rc, :]
        mu = jnp.mean(acc, axis=-1, keepdims=True)
        cen = acc - mu
        var = jnp.mean(cen * cen, axis=-1, keepdims=True)
        yn = cen * lax.rsqrt(var + EPS) * lng_ref[...] + lnb_ref[...]
        ybuf[r0:r0 + rc, :] = (yn * jax.nn.sigmoid(yn)).astype(BF16)
    y_conv = jnp.dot(ybuf[...], wco_ref[...], preferred_element_type=F32)

    t = i * tm + lax.broadcasted_iota(jnp.int32, (tm, 1), 0)
    for g, w in enumerate(POOL_WINDOWS):
        cols = slice(g * POOL_GROUP_DIM, (g + 1) * POOL_GROUP_DIM)
        s = pbuf[HALO - w // 2:HALO - w // 2 + tm, cols]
        for j in range(1, w):
            s = s + pbuf[HALO - w // 2 + j:HALO - w // 2 + j + tm, cols]
        cnt = (jnp.minimum(t + w // 2, seq_len) - jnp.maximum(t - w // 2, 0)).astype(F32)
        mbuf[:, cols] = (s / cnt - pbuf[HALO:HALO + tm, cols]).astype(BF16)
    y_pool = jnp.dot(mbuf[...], wpool_ref[...], preferred_element_type=F32) * pscale_ref[...]
    y_pool = jnp.dot(y_pool.astype(BF16), wpo_ref[...], preferred_element_type=F32)

    y_mla = jnp.dot(o_ref[0], wmo_ref[...], preferred_element_type=F32)

    mixed = (gates[:, 0:d] * y_conv + gates[:, d:2 * d] * y_pool + gates[:, 2 * d:3 * d] * y_mla)
    x1 = x + gate1 * jnp.dot(mixed.astype(BF16), wout_ref[...], preferred_element_type=F32)
    x1_ref[0] = x1

    h2 = x1 * _rms(x1, d) * g2_ref[...] * (1.0 + scale2) + shift2
    hi = h2.astype(BF16)
    h2e_ref[:, 0:d] = h2
    lo = (h2 - hi.astype(F32)).astype(BF16)
    r1 = jnp.dot(hi, wr1_ref[...], preferred_element_type=F32)
    r2 = jnp.dot(lo, wr2_ref[...], preferred_element_type=F32)
    logits = r1[:, 0:LANES] + r1[:, LANES:2 * LANES] + r2 + br_ref[...]

    lane = lax.broadcasted_iota(jnp.int32, (tm, LANES), 1)
    is_g = (lane >= N_EXPERTS) & (lane < N_EXPERTS + N_GROUPS)
    lg = jnp.where(is_g, logits, NEG_BIG)
    mg = jnp.max(lg, axis=-1, keepdims=True)
    eg = jnp.where(is_g, jnp.exp(lg - mg), 0.0)
    pg = eg / jnp.sum(eg, axis=-1, keepdims=True)
    pg_max = jnp.max(pg, axis=-1, keepdims=True)
    g_idx = jnp.min(jnp.where(is_g & (pg == pg_max), lane, 4 * LANES), axis=-1,
                    keepdims=True) - N_EXPERTS
    in_grp = (lane < N_EXPERTS) & ((lane >> 3) == g_idx)
    le = jnp.where(in_grp, logits, NEG_BIG)
    me = jnp.max(le, axis=-1, keepdims=True)
    ee = jnp.where(in_grp, jnp.exp(le - me), 0.0)
    pe = ee / jnp.sum(ee, axis=-1, keepdims=True)
    pe_m = jnp.where(in_grp, pe, -1.0)
    p1 = jnp.max(pe_m, axis=-1, keepdims=True)
    i1 = jnp.min(jnp.where(pe_m == p1, lane, 4 * LANES), axis=-1, keepdims=True)
    pe_m2 = jnp.where(lane == i1, -1.0, pe_m)
    p2 = jnp.max(pe_m2, axis=-1, keepdims=True)
    i2 = jnp.min(jnp.where(pe_m2 == p2, lane, 4 * LANES), axis=-1, keepdims=True)
    tot = p1 + p2
    w1 = pg_max * (p1 / tot)
    w2 = pg_max * (p2 / tot)
    first_low = i1 < i2
    e_lo = jnp.minimum(i1, i2) - g_idx * EXP_PER_GROUP
    e_hi = jnp.maximum(i1, i2) - g_idx * EXP_PER_GROUP
    cls = (g_idx * (EXP_PER_GROUP * EXP_PER_GROUP) + e_lo * EXP_PER_GROUP + e_hi).astype(F32)
    h2e_ref[:, d:d + LANES] = jnp.where(
        lane == 0, jnp.where(first_low, w1, w2),
        jnp.where(lane == 1, jnp.where(first_low, w2, w1), jnp.where(lane == 2, cls, 0.0)))


def _merge(x, mod6, g1, g2, u, pool, o, w, tm, h2e_prev, row0):
    B, S, D = x.shape
    Bm = mod6.shape[0]
    nh = tm // HALO
    nsb = S // tm
    blk0 = row0 // tm
    last_h = S // HALO - 1
    n_rows = h2e_prev.shape[0]
    mod_idx = (lambda b, i: (b, 0, 0)) if Bm > 1 else (lambda b, i: (0, 0, 0))

    def tok(width):
        return pl.BlockSpec((1, tm, width), lambda b, i: (b, i, 0))

    def prev(width):
        return pl.BlockSpec((1, HALO, width), lambda b, i: (b, jnp.maximum(i * nh - 1, 0), 0))

    def nxt(width):
        return pl.BlockSpec((1, HALO, width), lambda b, i: (b, jnp.minimum((i + 1) * nh, last_h), 0))

    weights = [w["wg"], w["dw"], w["dwb"], w["lng"], w["lnb"], w["wco"], w["wpool"], w["pscale"],
               w["wpo"], w["wmo"], w["wout"], w["wr1"], w["wr2"], w["br"]]
    in_specs = ([tok(D), pl.BlockSpec((1, 6, D), mod_idx), _const_spec((1, D)), _const_spec((1, D)),
                 tok(CONV_DIM), prev(CONV_DIM), nxt(CONV_DIM),
                 tok(POOL_DIM), prev(POOL_DIM), nxt(POOL_DIM), tok(HEADS * V_DIM)]
                + [_const_spec(a.shape) for a in weights] + [pl.BlockSpec(memory_space=pl.ANY)])
    args = [x, mod6, g1, g2, u, u, u, pool, pool, pool, o, *weights, h2e_prev]
    aliases = {len(args) - 1: 1}
    return pl.pallas_call(
        functools.partial(_merge_kernel, seq_len=S),
        grid=(B, nsb),
        in_specs=in_specs,
        out_specs=[tok(D), pl.BlockSpec((tm, D + LANES), lambda b, i: (blk0 + b * nsb + i, 0))],
        out_shape=[jax.ShapeDtypeStruct((B, S, D), F32),
                   jax.ShapeDtypeStruct((n_rows, D + LANES), F32)],
        scratch_shapes=[pltpu.VMEM((SUBLANES, tm + 2 * HALO, CONV_DIM), F32),
                        pltpu.VMEM((tm + 2 * HALO, POOL_DIM), F32),
                        pltpu.VMEM((tm, CONV_DIM), BF16),
                        pltpu.VMEM((tm, POOL_DIM), BF16)],
        input_output_aliases=aliases,
        compiler_params=pltpu.CompilerParams(
            dimension_semantics=("parallel", "arbitrary"), vmem_limit_bytes=VMEM_LIMIT),
        name="merge",
    )(*args)


def _route_kernel(cls_ref, ones_ref, utri_ref, ltri_ref, dest_ref, tile_ref, cnt, carry, off):
    ph = pl.program_id(0)
    j = pl.program_id(1)
    tb = cls_ref.shape[2]
    n_cls = cnt.shape[0]
    ntp = tile_ref.shape[1]
    cls_row = cls_ref[0]
    onehot = jnp.where(lax.broadcasted_iota(jnp.int32, (n_cls, tb), 0) == cls_row, 1.0, 0.0)
    oh = onehot.astype(BF16)
    blk = jnp.dot(oh, ones_ref[...], preferred_element_type=F32)

    @pl.when((ph == 0) & (j == 0))
    def _():
        cnt[...] = jnp.zeros_like(cnt)

    @pl.when(ph == 0)
    def _():
        cnt[...] += blk
        dest_ref[0, 0] = jnp.zeros((1, tb), jnp.int32)

    @pl.when((ph == 1) & (j == 0))
    def _():
        padded = jnp.floor((cnt[...] + (ROW_TILE - 1)) * (1.0 / ROW_TILE)) * ROW_TILE
        start = jnp.dot(ltri_ref[...], padded, preferred_element_type=F32,
                        precision=lax.Precision.HIGHEST)
        off[...] = start
        carry[...] = jnp.zeros_like(carry)
        end = jnp.concatenate([start + padded] * (ntp // LANES), axis=1)
        tile_start = (lax.broadcasted_iota(jnp.int32, (n_cls, ntp), 1) * ROW_TILE).astype(F32)
        tile_ref[...] = jnp.sum(jnp.where(end <= tile_start, 1.0, 0.0), axis=0,
                                keepdims=True).astype(jnp.int32)

    @pl.when(ph == 1)
    def _():
        excl = jnp.dot(oh, utri_ref[...], preferred_element_type=F32)
        base = jnp.concatenate([off[...] + carry[...]] * (tb // LANES), axis=1)
        dest_ref[0, 0] = jnp.sum(onehot * (excl + base), axis=0, keepdims=True).astype(jnp.int32)
        carry[...] += blk


def _route(cls, n_tiles_pad, tb):
    T = cls.shape[0]
    nblk = T // tb
    ones = jnp.ones((tb, LANES), BF16)
    utri = jnp.triu(jnp.ones((tb, tb), F32), k=1).astype(BF16)
    ltri = jnp.tril(jnp.ones((N_CLASSES, N_CLASSES), F32), k=-1)
    dest, tile_cls = pl.pallas_call(
        _route_kernel,
        grid=(2, nblk),
        in_specs=[pl.BlockSpec((1, 1, tb), lambda ph, j: (j, 0, 0)),
                  _const_spec(ones.shape), _const_spec(utri.shape), _const_spec(ltri.shape)],
        out_specs=[pl.BlockSpec((1, 1, 1, tb), lambda ph, j: (ph, j, 0, 0)),
                   pl.BlockSpec((1, n_tiles_pad), lambda ph, j: (0, 0))],
        out_shape=[jax.ShapeDtypeStruct((2, nblk, 1, tb), jnp.int32),
                   jax.ShapeDtypeStruct((1, n_tiles_pad), jnp.int32)],
        scratch_shapes=[pltpu.VMEM((N_CLASSES, LANES), F32)] * 3,
        compiler_params=pltpu.CompilerParams(
            dimension_semantics=("arbitrary", "arbitrary"), vmem_limit_bytes=VMEM_LIMIT),
        name="route",
    )(cls.reshape(nblk, 1, tb), ones, utri, ltri)
    return dest[1].reshape(T), tile_cls.reshape(n_tiles_pad)


def _row_copy_in(src_hbm, buf, sem, ids_ref, slot, n_rows, limit):
    for r in range(n_rows):
        src = jnp.minimum(ids_ref[0, 0, r], limit)
        pltpu.make_async_copy(src_hbm.at[pl.ds(src, 1), :], buf.at[slot, pl.ds(r, 1), :],
                              sem.at[slot]).start(priority=r % 2)


def _rows_wait(src_hbm, buf, sem, slot, n_rows):
    pltpu.make_async_copy(src_hbm.at[pl.ds(0, n_rows), :], buf.at[slot], sem.at[slot]).wait()


def _experts_kernel(elo_ref, ehi_ref, nused_ref, idc_ref, idn_ref, h_hbm,
                    wg1_ref, wu1_ref, wd1_ref, wg2_ref, wu2_ref, wd2_ref,
                    y_ref, xbuf, gsem, *, n_tok):
    i = pl.program_id(0)
    n_used = nused_ref[0]
    slot = i % 2
    other = 1 - slot
    d = y_ref.shape[1]

    @pl.when(i == 0)
    def _():
        _row_copy_in(h_hbm, xbuf, gsem, idc_ref, 0, ROW_TILE, n_tok - 1)

    @pl.when(i < n_used)
    def _():
        _row_copy_in(h_hbm, xbuf, gsem, idn_ref, other, ROW_TILE, n_tok - 1)
        _rows_wait(h_hbm, xbuf, gsem, slot, ROW_TILE)
        x = xbuf[slot]
        hb = x[:, 0:d].astype(BF16)
        w_lo = x[:, d:d + 1]
        w_hi = x[:, d + 1:d + 2]
        a = jnp.dot(hb, wg1_ref[0], preferred_element_type=F32)
        u = jnp.dot(hb, wu1_ref[0], preferred_element_type=F32)
        hid = ((a * jax.nn.sigmoid(a)) * u * w_lo).astype(BF16)
        y = jnp.dot(hid, wd1_ref[0], preferred_element_type=F32)
        a = jnp.dot(hb, wg2_ref[0], preferred_element_type=F32)
        u = jnp.dot(hb, wu2_ref[0], preferred_element_type=F32)
        hid = ((a * jax.nn.sigmoid(a)) * u * w_hi).astype(BF16)
        y_ref[...] = y + jnp.dot(hid, wd2_ref[0], preferred_element_type=F32)

    @pl.when(i == n_used - 1)
    def _():
        _rows_wait(h_hbm, xbuf, gsem, other, ROW_TILE)

    @pl.when(i >= n_used)
    def _():
        y_ref[...] = jnp.zeros_like(y_ref)


def _experts(h2e, ids, e_lo, e_hi, n_used, wg, wu, wd, n_tok):
    D = h2e.shape[1] - LANES
    E, _, F = wg.shape
    n_tiles = ids.shape[0] - 1

    def ids_spec(shift):
        return pl.BlockSpec((1, 1, ROW_TILE), lambda i, lo, hi, nu: (i + shift, 0, 0),
                            memory_space=pltpu.SMEM)

    def w_spec(shape, which):
        if which == 0:
            return pl.BlockSpec(shape, lambda i, lo, hi, nu: (lo[i], 0, 0))
        return pl.BlockSpec(shape, lambda i, lo, hi, nu: (hi[i], 0, 0))

    grid_spec = pltpu.PrefetchScalarGridSpec(
        num_scalar_prefetch=3,
        grid=(n_tiles,),
        in_specs=[ids_spec(0), ids_spec(1), pl.BlockSpec(memory_space=pl.ANY),
                  w_spec((1, D, F), 0), w_spec((1, D, F), 0), w_spec((1, F, D), 0),
                  w_spec((1, D, F), 1), w_spec((1, D, F), 1), w_spec((1, F, D), 1)],
        out_specs=pl.BlockSpec((ROW_TILE, D), lambda i, lo, hi, nu: (i, 0)),
        scratch_shapes=[pltpu.VMEM((2, ROW_TILE, D + LANES), F32),
                        pltpu.SemaphoreType.DMA((2,))])
    return pl.pallas_call(
        functools.partial(_experts_kernel, n_tok=n_tok),
        grid_spec=grid_spec,
        out_shape=jax.ShapeDtypeStruct((n_tiles * ROW_TILE, D), F32),
        compiler_params=pltpu.CompilerParams(
            dimension_semantics=("arbitrary",), vmem_limit_bytes=VMEM_LIMIT),
        name="experts",
    )(e_lo, e_hi, n_used, ids, ids, h2e, wg, wu, wd, wg, wu, wd)


def _resid_kernel(dc_ref, dn_ref, x1_ref, mod_ref, y_hbm, o_ref, buf, sem):
    i = pl.program_id(0)
    slot = i % 2
    other = 1 - slot
    tm = x1_ref.shape[0]
    limit = y_hbm.shape[0] - 1

    @pl.when(i == 0)
    def _():
        _row_copy_in(y_hbm, buf, sem, dc_ref, 0, tm, limit)

    @pl.when(i + 1 < pl.num_programs(0))
    def _():
        _row_copy_in(y_hbm, buf, sem, dn_ref, other, tm, limit)

    _rows_wait(y_hbm, buf, sem, slot, tm)
    o_ref[...] = x1_ref[...] + mod_ref[0, 5:6, :] * buf[slot]


def _resid(x1, y, dest, mod6, row0, tm):
    B, S, D = x1.shape
    Bm = mod6.shape[0]
    nsb = S // tm
    nblk = B * nsb
    blk0 = row0 // tm
    dest3 = dest.reshape(-1, 1, tm)
    last = blk0 + nblk - 1
    mod_idx = (lambda i: (lax.div(i, jnp.int32(nsb)), 0, 0)) if Bm > 1 else (lambda i: (0, 0, 0))
    out = pl.pallas_call(
        _resid_kernel,
        grid=(nblk,),
        in_specs=[pl.BlockSpec((1, 1, tm), lambda i: (jnp.minimum(blk0 + i, last), 0, 0),
                               memory_space=pltpu.SMEM),
                  pl.BlockSpec((1, 1, tm), lambda i: (jnp.minimum(blk0 + i + 1, last), 0, 0),
                               memory_space=pltpu.SMEM),
                  pl.BlockSpec((tm, D), lambda i: (i, 0)),
                  pl.BlockSpec((1, 6, D), mod_idx),
                  pl.BlockSpec(memory_space=pl.ANY)],
        out_specs=pl.BlockSpec((tm, D), lambda i: (i, 0)),
        out_shape=jax.ShapeDtypeStruct((B * S, D), F32),
        scratch_shapes=[pltpu.VMEM((2, tm, D), F32), pltpu.SemaphoreType.DMA((2,))],
        compiler_params=pltpu.CompilerParams(
            dimension_semantics=("arbitrary",), vmem_limit_bytes=VMEM_LIMIT),
        name="resid",
    )(dest3, dest3, x1.reshape(B * S, D), mod6, y)
    return out.reshape(B, S, D)


def _rope_tables(seq_len, n_ctx):
    rows = seq_len // GRID_W
    row = jnp.repeat(jnp.arange(rows), GRID_W).astype(F32)
    col = jnp.tile(jnp.arange(GRID_W), rows).astype(F32)
    half = ROPE // 2
    inv = 1.0 / (ROPE_BASE ** (jnp.arange(0, half, 2, dtype=F32) / half))
    ang = jnp.concatenate([row[:, None] * inv, col[:, None] * inv], axis=-1)
    cos, sin = jnp.cos(ang), jnp.sin(ang)
    q4 = ROPE // 4
    zero = jnp.zeros((seq_len, q4), F32)
    c_parts, s1_parts, s2_parts = [jnp.ones((seq_len, NOPE), F32)], [jnp.zeros((seq_len, NOPE), F32)], \
        [jnp.zeros((seq_len, NOPE), F32)]
    for hf in range(2):
        cs, sn = cos[:, hf * q4:(hf + 1) * q4], sin[:, hf * q4:(hf + 1) * q4]
        c_parts += [cs, cs]
        s1_parts += [zero, sn]
        s2_parts += [-sn, zero]
    pad = LANES - NOPE - ROPE
    c_parts.append(jnp.ones((seq_len, pad), F32))
    s1_parts.append(jnp.zeros((seq_len, pad), F32))
    s2_parts.append(jnp.zeros((seq_len, pad), F32))
    c, s1, s2 = (jnp.concatenate(p, axis=-1) for p in (c_parts, s1_parts, s2_parts))
    if n_ctx:
        c = jnp.concatenate([c, jnp.ones((n_ctx, LANES), F32)], axis=0)
        s1 = jnp.concatenate([s1, jnp.zeros((n_ctx, LANES), F32)], axis=0)
        s2 = jnp.concatenate([s2, jnp.zeros((n_ctx, LANES), F32)], axis=0)
    return c, s1, s2


def _rope_partner():
    lane = jnp.arange(LANES)
    q4 = ROPE // 4
    in_rope = (lane >= NOPE) & (lane < NOPE + ROPE)
    first = ((lane - NOPE) // q4) % 2 == 0
    partner = jnp.where(in_rope, jnp.where(first, lane + q4, lane - q4), lane)
    return partner, in_rope.astype(F32)


def _head_matrices():
    row_head = jnp.arange(HEADS * LANES) // LANES
    hsum = (row_head[:, None] == jnp.arange(LANES)[None, :]).astype(BF16)
    hbc = jnp.concatenate([hsum.T, hsum.T], axis=0)
    return hsum, hbc


def _layer_weights(l, p):
    d = p["w_in"].shape[1]
    w_in = p["w_in"][l]
    n_a = 2 * CONV_DIM + POOL_DIM + Q_RANK + KV_RANK
    wa = jnp.concatenate([w_in[:, :n_a], jnp.zeros((d, NOPE), F32), w_in[:, n_a:n_a + ROPE],
                          jnp.zeros((d, LANES - NOPE - ROPE), F32)], axis=1).astype(BF16)
    wg = w_in[:, n_a + ROPE:].astype(BF16)
    wuq3 = jnp.pad(p["w_uq"][l].reshape(Q_RANK, HEADS, QK_DIM), ((0, 0), (0, 0), (0, LANES - QK_DIM)))
    wuq = wuq3.reshape(Q_RANK, HEADS * LANES).astype(BF16)
    partner, is_rope = _rope_partner()
    wuq_sw = (wuq3[:, :, partner] * is_rope).reshape(Q_RANK, HEADS * LANES).astype(BF16)
    qgain = jnp.pad(p["q_head_g"][l], (0, LANES - QK_DIM))
    wukv = p["w_ukv"][l].reshape(KV_RANK, HEADS, NOPE + V_DIM)
    wuk = jnp.pad(wukv[:, :, :NOPE], ((0, 0), (0, 0), (0, LANES - NOPE))).reshape(
        KV_RANK, HEADS * LANES).astype(BF16)
    wuv = wukv[:, :, NOPE:].reshape(KV_RANK, HEADS * V_DIM).astype(BF16)
    pad_gain = lambda g: jnp.pad(g, (0, LANES - QK_DIM)).reshape(1, LANES)
    wpool = jnp.zeros((POOL_DIM, POOL_DIM), F32)
    for g in range(len(POOL_WINDOWS)):
        sl = slice(g * POOL_GROUP_DIM, (g + 1) * POOL_GROUP_DIM)
        wpool = wpool.at[sl, sl].set(p["w_pool"][l, g])
    wr = jnp.concatenate([p["w_router_e"][l], p["w_router_g"][l]], axis=1)
    wr = jnp.pad(wr, ((0, 0), (0, LANES - wr.shape[1])))
    wr_hi = wr.astype(BF16)
    wr_lo = (wr - wr_hi.astype(F32)).astype(BF16)
    br = jnp.concatenate([p["b_router_e"][l], p["b_router_g"][l]])
    br = jnp.pad(br, (0, LANES - br.shape[0])).reshape(1, LANES)
    return dict(
        g1=p["norm1_g"][l].reshape(1, d), g2=p["norm2_g"][l].reshape(1, d),
        wa=wa, wg=wg, wuq=wuq, wuq_rope=jnp.concatenate([wuq, wuq_sw], axis=1),
        qgain_partner=qgain[partner].reshape(1, LANES), wuk=wuk, wuv=wuv,
        qng=p["q_norm_g"][l].reshape(1, Q_RANK), kvg=p["kv_norm_g"][l].reshape(1, KV_RANK),
        qgain=pad_gain(p["q_head_g"][l]), kgain=pad_gain(p["k_head_g"][l]),
        dw=p["conv_dw"][l], dwb=p["conv_dw_b"][l].reshape(1, CONV_DIM),
        lng=p["conv_ln_g"][l].reshape(1, CONV_DIM), lnb=p["conv_ln_b"][l].reshape(1, CONV_DIM),
        wco=p["w_conv_out"][l].astype(BF16), wpool=wpool.astype(BF16),
        pscale=p["pool_scale"][l].reshape(1, POOL_DIM), wpo=p["w_pool_out"][l].astype(BF16),
        wmo=p["w_mla_out"][l].astype(BF16), wout=p["w_out"][l].astype(BF16),
        wr1=jnp.concatenate([wr_hi, wr_lo], axis=1), wr2=wr_hi, br=br,
        weg=p["w_exp_gate"][l].astype(BF16), weu=p["w_exp_up"][l].astype(BF16),
        wed=p["w_exp_down"][l].astype(BF16))


def _pick(n, pref):
    t = min(n, pref)
    while n % t:
        t //= 2
    return t


def _mixers(x, mod6, w, ctx_ckv, ctx_krp, rope_k, head_mats, h2e_prev, row0):
    B, S, D = x.shape
    tm = _pick(S, 512)
    if rope_k is not None:
        c, s1, s2 = (t[:S] for t in rope_k)
        rope_q = (c * w["qgain"], (s1 + s2) * w["qgain_partner"])
        wuq = w["wuq_rope"]
    else:
        rope_q, wuq = None, w["wuq"]
    u, pool, q, ckv, krp = _pre(x, mod6, w["g1"], w["wa"], w["qng"], w["kvg"], wuq, w["qgain"],
                                *head_mats, rope_q, tm)
    if ctx_ckv is not None:
        ckv_all = jnp.concatenate([ckv, ctx_ckv], axis=1)
        krp_all = jnp.concatenate([krp, ctx_krp], axis=1)
    else:
        ckv_all, krp_all = ckv, krp
    Sk = ckv_all.shape[1]
    k, v = _kv(ckv_all, krp_all, w["wuk"], w["wuv"], w["kgain"], rope_k, _pick(Sk, 1088))
    o = _attn(q, k, v, tm, 512)
    x1, h2e = _merge(x, mod6, w["g1"], w["g2"], u, pool, o, w, _pick(S, 256), h2e_prev, row0)
    return x1, h2e, ckv, krp


def _moe(h2e, w):
    T = h2e.shape[0]
    D = h2e.shape[1] - LANES
    cls = h2e[:, D + 2].astype(jnp.int32)
    n_tiles = T // ROW_TILE + N_USED_CLASSES
    n_tiles_pad = -(-n_tiles // LANES) * LANES
    dest, tile_cls = _route(cls, n_tiles_pad, _pick(T, 512))
    tile_cls = tile_cls[:n_tiles]
    valid = tile_cls < N_CLASSES
    grp = tile_cls // (EXP_PER_GROUP * EXP_PER_GROUP)
    e_lo = grp * EXP_PER_GROUP + (tile_cls // EXP_PER_GROUP) % EXP_PER_GROUP
    e_hi = grp * EXP_PER_GROUP + tile_cls % EXP_PER_GROUP
    n_used = jnp.sum(valid.astype(jnp.int32))
    last = jnp.maximum(n_used - 1, 0)
    e_lo = jnp.where(valid, e_lo, e_lo[last])
    e_hi = jnp.where(valid, e_hi, e_hi[last])
    ids = jnp.full(((n_tiles + 1) * ROW_TILE,), T, jnp.int32).at[dest].set(
        jnp.arange(T, dtype=jnp.int32), unique_indices=True, mode="promise_in_bounds")
    y = _experts(h2e, ids.reshape(n_tiles + 1, 1, ROW_TILE), e_lo, e_hi, n_used.reshape(1),
                 w["weg"], w["weu"], w["wed"], T)
    return y, dest


def kernel(x_prompt, x_sample, cache_ckv, cache_kr, c, c_ctx, norm1_g, norm2_g, w_ada, b_ada, w_in, conv_dw, conv_dw_b, conv_ln_g, conv_ln_b, w_conv_out, w_pool, pool_scale, w_pool_out, q_norm_g, w_uq, kv_norm_g, w_ukv, q_head_g, k_head_g, w_mla_out, w_out, w_router_g, b_router_g, w_router_e, b_router_e, w_exp_gate, w_exp_up, w_exp_down):
    p = dict(norm1_g=norm1_g, norm2_g=norm2_g, w_in=w_in, conv_dw=conv_dw, conv_dw_b=conv_dw_b,
             conv_ln_g=conv_ln_g, conv_ln_b=conv_ln_b, w_conv_out=w_conv_out, w_pool=w_pool,
             pool_scale=pool_scale, w_pool_out=w_pool_out, q_norm_g=q_norm_g, w_uq=w_uq,
             kv_norm_g=kv_norm_g, w_ukv=w_ukv, q_head_g=q_head_g, k_head_g=k_head_g,
             w_mla_out=w_mla_out, w_out=w_out, w_router_g=w_router_g, b_router_g=b_router_g,
             w_router_e=w_router_e, b_router_e=b_router_e, w_exp_gate=w_exp_gate,
             w_exp_up=w_exp_up, w_exp_down=w_exp_down)
    depth, d = norm1_g.shape
    db, ds, _ = x_sample.shape
    past = cache_ckv.shape[2]

    rows = -(-(1 + db) // 8) * 8
    cvec = jnp.zeros((rows, d), F32).at[0].set(c_ctx).at[1:1 + db].set(c)
    mod = _ada(cvec, w_ada, b_ada)

    rope_k = _rope_tables(ds, past)
    head_mats = _head_matrices()
    cache_krp = jnp.pad(cache_kr, ((0, 0), (0, 0), (0, 0), (NOPE, LANES - NOPE - ROPE)))

    y_prompt, y_sample = x_prompt, x_sample
    t_ctx = x_prompt.shape[0] * x_prompt.shape[1]
    t_all = t_ctx + db * ds
    new_ckv, new_kr = [], []
    for l in range(depth):
        w = _layer_weights(l, p)
        mod_ctx = mod[l, 0:1].reshape(1, 6, d)
        mod_lat = mod[l, 1:1 + db].reshape(db, 6, d)
        x1_p, h2e, ckv_l, krp_l = _mixers(y_prompt, mod_ctx, w, None, None, None, head_mats,
                                          jnp.zeros((t_all, d + LANES), F32), 0)
        new_ckv.append(ckv_l)
        new_kr.append(krp_l[:, :, NOPE:NOPE + ROPE])
        x1_s, h2e, _, _ = _mixers(y_sample, mod_lat, w, cache_ckv[:, l], cache_krp[:, l],
                                  rope_k, head_mats, h2e, t_ctx)
        y, dest = _moe(h2e, w)
        y_prompt = _resid(x1_p, y, dest, mod_ctx, 0, RESID_ROWS)
        y_sample = _resid(x1_s, y, dest, mod_lat, t_ctx, RESID_ROWS)
    return (y_prompt, y_sample, jnp.stack(new_ckv, axis=1), jnp.stack(new_kr, axis=1))
```
